```python
import jax, jax.numpy as jnp
from jax import lax
import numpy as np

D_MODEL = 1024
BATCH = 16
SEQ = 256
DEPTH = 1
DEC_BATCH = 4
DEC_SEQ = 2048
PAST_LEN = 512

GRID_W = 64
NORM_EPS = 1e-6
SSM_EXPAND = 2
SSM_D_INNER = SSM_EXPAND * D_MODEL
SSM_HEADDIM = 64
SSM_HEADS = SSM_D_INNER // SSM_HEADDIM
SSM_GROUPS = 4
SSM_STATE = 128
SSM_CONV = 3
SSM_CHUNK = 128
WKV_WIDTH = D_MODEL
WKV_HEADSIZE = 64
WKV_HEADS = WKV_WIDTH // WKV_HEADSIZE
WKV_DECAY_RANK = 64
WKV_A_RANK = 64
WKV_GN_EPS = 64e-5
XBC_W = SSM_D_INNER + 2 * SSM_GROUPS * SSM_STATE
RWKV_SCAN_W = 3 * WKV_WIDTH + WKV_DECAY_RANK + WKV_A_RANK
ZA_END = SSM_D_INNER
XBC_END = ZA_END + XBC_W
DT_END = XBC_END + SSM_HEADS
RW_END = DT_END + RWKV_SCAN_W
ZB_END = RW_END + WKV_WIDTH
IN_W = ZB_END + 2 * D_MODEL

kernel_name = "bidir_ssd_rwkv7_prefix_dit_step"


def rmsnorm(x, g):
    xf = x.astype(jnp.float32)
    y = xf * lax.rsqrt(jnp.mean(xf * xf, axis=-1, keepdims=True) + NORM_EPS)
    return (y * g.astype(jnp.float32)).astype(x.dtype)


def flip(t):
    return jnp.flip(t, axis=1)


def to_col_major(x):
    b, L, C = x.shape
    rows = L // GRID_W
    return x.reshape(b, rows, GRID_W, C).transpose(0, 2, 1, 3).reshape(b, L, C)


def from_col_major(x):
    b, L, C = x.shape
    rows = L // GRID_W
    return x.reshape(b, GRID_W, rows, C).transpose(0, 2, 1, 3).reshape(b, L, C)


def centred_dwconv(x, w, bias):
    pad = SSM_CONV // 2
    y = lax.conv_general_dilated(x, w[:, None, :].astype(x.dtype), window_strides=(1,),
                                 padding=[(pad, pad)], dimension_numbers=('NWC', 'WIO', 'NWC'),
                                 feature_group_count=x.shape[-1])
    return y + bias.astype(x.dtype)


def token_shift(p, mu):
    prev = jnp.pad(p, ((0, 0), (1, 0), (0, 0)))[:, :-1]
    nxt = jnp.pad(p, ((0, 0), (0, 1), (0, 0)))[:, 1:]
    return p + mu.astype(p.dtype) * (0.5 * (prev + nxt) - p)


def segsum(a):
    T = a.shape[-1]
    ae = jnp.broadcast_to(a[..., :, None], a.shape + (T,))
    ae = jnp.where(jnp.tril(jnp.ones((T, T), bool), -1), ae, 0.0)
    cs = jnp.cumsum(ae, axis=-2)
    return jnp.where(jnp.tril(jnp.ones((T, T), bool), 0), cs, -jnp.inf)


def ssd_scan(x, dt, A, B, C, s0):
    b, L, H, P = x.shape
    G, N = B.shape[-2], B.shape[-1]
    Hg = H // G
    Q = SSM_CHUNK
    nc = L // Q
    xc = x.reshape(b, nc, Q, H, P)
    dtc = dt.reshape(b, nc, Q, H)
    Bc = B.reshape(b, nc, Q, G, N)
    Cc = C.reshape(b, nc, Q, G, N)
    a = (dtc * A).transpose(0, 3, 1, 2)
    a_cs = jnp.cumsum(a, axis=-1)
    xdt = (xc * dtc[..., None]).reshape(b, nc, Q, G, Hg, P)
    Lm = jnp.exp(segsum(a)).reshape(b, G, Hg, nc, Q, Q)
    CB = jnp.einsum('bclgn,bcsgn->bcgls', Cc, Bc)
    y_diag = jnp.einsum('bcgls,bgjcls,bcsgjp->bclgjp', CB, Lm, xdt)
    decay_states = jnp.exp(a_cs[..., -1:] - a_cs).reshape(b, G, Hg, nc, Q)
    states = jnp.einsum('bcsgn,bgjcs,bcsgjp->bcgjpn', Bc, decay_states, xdt).reshape(b, nc, H, P, N)
    states_all = jnp.concatenate([s0[:, None].astype(jnp.float32), states], axis=1)
    chunk_decay = jnp.pad(a_cs[..., -1], ((0, 0), (0, 0), (1, 0)))
    dc = jnp.exp(segsum(chunk_decay))
    new_states = jnp.einsum('bhzc,bchpn->bzhpn', dc, states_all)
    prev = new_states[:, :-1].reshape(b, nc, G, Hg, P, N)
    s_final = new_states[:, -1]
    ea = jnp.exp(a_cs).reshape(b, G, Hg, nc, Q)
    y_off = jnp.einsum('bclgn,bcgjpn,bgjcl->bclgjp', Cc, prev, ea)
    return (y_diag + y_off).reshape(b, L, H, P), s_final


def wkv_scan(r, w, k, v, kneg, kka, s0):
    def step(S, inp):
        r_t, w_t, k_t, v_t, kn_t, ka_t = inp
        sa = jnp.einsum('bhvk,bhk->bhv', S, kn_t)
        S = S * w_t[:, :, None, :] + sa[..., None] * ka_t[:, :, None, :] + v_t[..., None] * k_t[:, :, None, :]
        return S, jnp.einsum('bhvk,bhk->bhv', S, r_t)
    xs = tuple(jnp.moveaxis(t, 1, 0) for t in (r, w, k, v, kneg, kka))
    s_final, o = lax.scan(step, s0.astype(jnp.float32), xs)
    return jnp.moveaxis(o, 0, 1), s_final


def mixer_layer(x, cond, grid, s_ssm_f, s_ssm_b, s_wkv_f, s_wkv_b,
                w_mod, b_mod, norm_g, w_in, conv_w, conv_b, a_log, dt_bias, d_skip, ssm_norm_g, p_a,
                shift_mu, w0, w2, a0, a2, k_k, k_a, r_k, ln_w, ln_b, p_b, w_out):
    f32 = jnp.float32
    b, L, _ = x.shape
    mod = jax.nn.silu(cond) @ w_mod + b_mod
    shift, scale, gate = jnp.split(mod, 3, axis=-1)
    h = rmsnorm(x, norm_g) * (1 + scale[:, None]) + shift[:, None]
    p = h @ w_in
    z_a, xbc, dt_raw, rw, z_b, gl = jnp.split(p, [ZA_END, XBC_END, DT_END, RW_END, ZB_END], axis=-1)

    xbc = jax.nn.silu(centred_dwconv(xbc, conv_w, conv_b))
    xs, Bm, Cm = jnp.split(xbc, [SSM_D_INNER, SSM_D_INNER + SSM_GROUPS * SSM_STATE], axis=-1)
    xs_h = xs.reshape(b, L, SSM_HEADS, SSM_HEADDIM).astype(f32)
    Bm = Bm.reshape(b, L, SSM_GROUPS, SSM_STATE).astype(f32)
    Cm = Cm.reshape(b, L, SSM_GROUPS, SSM_STATE).astype(f32)
    dt_raw = dt_raw.astype(f32)
    dtb = dt_bias.astype(f32)
    dt_f = jax.nn.softplus(dt_raw + dtb[0])
    dt_b = jax.nn.softplus(dt_raw + dtb[1])
    A = -jnp.exp(a_log.astype(f32))
    y_f, fs_ssm_f = ssd_scan(xs_h, dt_f, A[0], Bm, Cm, s_ssm_f)
    y_b, fs_ssm_b = ssd_scan(flip(xs_h), flip(dt_b), A[1], flip(Bm), flip(Cm), s_ssm_b)
    y = y_f + flip(y_b) + d_skip.astype(f32)[:, None] * xs_h
    y = y.reshape(b, L, SSM_D_INNER).astype(x.dtype)
    u_a = rmsnorm(y * jax.nn.silu(z_a), ssm_norm_g) @ p_a

    if grid:
        rw = to_col_major(rw)
    rw = token_shift(rw, shift_mu).astype(f32)
    r, k, v, wd, ad = jnp.split(rw, [WKV_WIDTH, 2 * WKV_WIDTH, 3 * WKV_WIDTH,
                                     3 * WKV_WIDTH + WKV_DECAY_RANK], axis=-1)
    a = jax.nn.sigmoid(a0.astype(f32) + ad @ a2.astype(f32))
    tw = jnp.tanh(wd)
    w0f, w2f = w0.astype(f32), w2.astype(f32)
    decay_f = jnp.exp(-jnp.exp(-jax.nn.softplus(-(w0f[0] + tw @ w2f[0])) - 0.5))
    decay_b = jnp.exp(-jnp.exp(-jax.nn.softplus(-(w0f[1] + tw @ w2f[1])) - 0.5))
    hs = (b, L, WKV_HEADS, WKV_HEADSIZE)
    kk = (k * k_k.astype(f32)).reshape(hs)
    kk = kk / jnp.maximum(jnp.sqrt(jnp.sum(kk * kk, axis=-1, keepdims=True)), 1e-12)
    k = k * (1 + (a - 1) * k_a.astype(f32))
    rh, kh, vh, ah = r.reshape(hs), k.reshape(hs), v.reshape(hs), a.reshape(hs)
    kneg, kka = -kk, kk * ah
    o_f, fs_wkv_f = wkv_scan(rh, decay_f.reshape(hs), kh, vh, kneg, kka, s_wkv_f)
    o_b, fs_wkv_b = wkv_scan(flip(rh), flip(decay_b.reshape(hs)), flip(kh), flip(vh),
                             flip(kneg), flip(kka), s_wkv_b)
    o = o_f + flip(o_b)
    mu = jnp.mean(o, axis=-1, keepdims=True)
    var = jnp.mean(jnp.square(o - mu), axis=-1, keepdims=True)
    o = ((o - mu) * lax.rsqrt(var + WKV_GN_EPS)).reshape(b, L, WKV_WIDTH)
    o = o * ln_w.astype(f32) + ln_b.astype(f32)
    bonus = jnp.sum(rh * kh * r_k.astype(f32).reshape(WKV_HEADS, WKV_HEADSIZE), axis=-1, keepdims=True) * vh
    o = o + bonus.reshape(b, L, WKV_WIDTH)
    if grid:
        o = from_col_major(o)
    u_b = (o.astype(x.dtype) * jax.nn.silu(z_b)) @ p_b

    g_a, g_b = jnp.split(jax.nn.sigmoid(gl), 2, axis=-1)
    out = (g_a * u_a + g_b * u_b) @ w_out
    x = x + gate[:, None] * out
    return x, (fs_ssm_f, fs_ssm_b, fs_wkv_f, fs_wkv_b)


def setup_inputs(seed: int = 0) -> dict:
    key = jax.random.key(seed)
    ks = jax.random.split(key, 40)
    f32 = jnp.float32

    def nrm(k, shape, s):
        return jax.random.normal(k, shape, f32) * s

    ssm_st = (DEC_BATCH, DEPTH, SSM_HEADS, SSM_HEADDIM, SSM_STATE)
    wkv_st = (DEC_BATCH, DEPTH, WKV_HEADS, WKV_HEADSIZE, WKV_HEADSIZE)
    dt0 = jnp.exp(jax.random.uniform(ks[13], (DEPTH, 2, SSM_HEADS), f32, np.log(1e-3), np.log(1e-1)))
    return {
        "x_prompt": nrm(ks[0], (BATCH, SEQ, D_MODEL), 1.0),
        "x_sample": nrm(ks[1], (DEC_BATCH, DEC_SEQ, D_MODEL), 1.0),
        "state_ssm_fwd": nrm(ks[2], ssm_st, 0.1),
        "state_ssm_bwd": nrm(ks[3], ssm_st, 0.1),
        "state_wkv_fwd": nrm(ks[4], wkv_st, 0.5),
        "state_wkv_bwd": nrm(ks[5], wkv_st, 0.5),
        "c": nrm(ks[6], (DEC_BATCH, D_MODEL), 1.0),
        "c_ctx": nrm(ks[7], (D_MODEL,), 1.0),
        "w_mod": nrm(ks[8], (DEPTH, D_MODEL, 3 * D_MODEL), 0.5 * D_MODEL ** -0.5),
        "b_mod": nrm(ks[9], (DEPTH, 3 * D_MODEL), 0.02),
        "norm_g": 1.0 + nrm(ks[10], (DEPTH, D_MODEL), 0.02),
        "w_in": nrm(ks[11], (DEPTH, D_MODEL, IN_W), D_MODEL ** -0.5),
        "conv_w": nrm(ks[12], (DEPTH, SSM_CONV, XBC_W), SSM_CONV ** -0.5),
        "conv_b": nrm(ks[14], (DEPTH, XBC_W), 0.02),
        "a_log": jnp.log(jax.random.uniform(ks[15], (DEPTH, 2, SSM_HEADS), f32, 1.0, 16.0)),
        "dt_bias": dt0 + jnp.log(-jnp.expm1(-dt0)),
        "d_skip": 1.0 + nrm(ks[16], (DEPTH, SSM_HEADS), 0.02),
        "ssm_norm_g": 1.0 + nrm(ks[17], (DEPTH, SSM_D_INNER), 0.02),
        "p_a": nrm(ks[18], (DEPTH, SSM_D_INNER, D_MODEL), SSM_D_INNER ** -0.5),
        "shift_mu": jax.random.uniform(ks[19], (DEPTH, RWKV_SCAN_W), f32, 0.0, 1.0),
        "w0": jax.random.uniform(ks[20], (DEPTH, 2, WKV_WIDTH), f32, -6.0, -1.0),
        "w2": nrm(ks[21], (DEPTH, 2, WKV_DECAY_RANK, WKV_WIDTH), 0.5 * WKV_DECAY_RANK ** -0.5),
        "a0": nrm(ks[22], (DEPTH, WKV_WIDTH), 0.1),
        "a2": nrm(ks[23], (DEPTH, WKV_A_RANK, WKV_WIDTH), 0.5 * WKV_A_RANK ** -0.5),
        "k_k": 0.85 + nrm(ks[24], (DEPTH, WKV_WIDTH), 0.02),
        "k_a": 1.0 + nrm(ks[25], (DEPTH, WKV_WIDTH), 0.02),
        "r_k": nrm(ks[26], (DEPTH, WKV_WIDTH), 0.1),
        "ln_w": 1.0 + nrm(ks[27], (DEPTH, WKV_WIDTH), 0.02),
        "ln_b": nrm(ks[28], (DEPTH, WKV_WIDTH), 0.02),
        "p_b": nrm(ks[29], (DEPTH, WKV_WIDTH, D_MODEL), WKV_WIDTH ** -0.5),
        "w_out": nrm(ks[30], (DEPTH, D_MODEL, D_MODEL), D_MODEL ** -0.5),
        "final_g": 1.0 + nrm(ks[31], (D_MODEL,), 0.02),
    }


def reference(x_prompt, x_sample, state_ssm_fwd, state_ssm_bwd, state_wkv_fwd, state_wkv_bwd, c, c_ctx,
              w_mod, b_mod, norm_g, w_in, conv_w, conv_b, a_log, dt_bias, d_skip, ssm_norm_g, p_a,
              shift_mu, w0, w2, a0, a2, k_k, k_a, r_k, ln_w, ln_b, p_b, w_out, final_g):
    f32 = jnp.float32
    params = (w_mod, b_mod, norm_g, w_in, conv_w, conv_b, a_log, dt_bias, d_skip, ssm_norm_g, p_a,
              shift_mu, w0, w2, a0, a2, k_k, k_a, r_k, ln_w, ln_b, p_b, w_out)

    bp = x_prompt.shape[0]
    z_ssm = jnp.zeros((bp, SSM_HEADS, SSM_HEADDIM, SSM_STATE), f32)
    z_wkv = jnp.zeros((bp, WKV_HEADS, WKV_HEADSIZE, WKV_HEADSIZE), f32)
    cond_ctx = c_ctx[None, :]
    xp = x_prompt
    st_sf, st_sb, st_wf, st_wb = [], [], [], []
    for l in range(DEPTH):
        lp = tuple(t[l] for t in params)
        xp, (sf, sb, wf, wb) = mixer_layer(xp, cond_ctx, False, z_ssm, z_ssm, z_wkv, z_wkv, *lp)
        st_sf.append(sf)
        st_sb.append(sb)
        st_wf.append(wf)
        st_wb.append(wb)
    y_prompt = rmsnorm(xp, final_g)
    new_ssm_fwd = jnp.stack(st_sf, axis=1)
    new_ssm_bwd = jnp.stack(st_sb, axis=1)
    new_wkv_fwd = jnp.stack(st_wf, axis=1)
    new_wkv_bwd = jnp.stack(st_wb, axis=1)

    xs = x_sample
    for l in range(DEPTH):
        lp = tuple(t[l] for t in params)
        xs, _ = mixer_layer(xs, c, True, state_ssm_fwd[:, l], state_ssm_bwd[:, l],
                            state_wkv_fwd[:, l], state_wkv_bwd[:, l], *lp)
    y_sample = rmsnorm(xs, final_g)

    return (y_prompt, y_sample, new_ssm_fwd, new_ssm_bwd, new_wkv_fwd, new_wkv_bwd)
```

```python
import functools
import math

import jax
import jax.numpy as jnp
from jax import lax
from jax.experimental import pallas as pl
from jax.experimental.pallas import tpu as pltpu

F32 = jnp.float32
BF16 = jnp.bfloat16
HI = lax.Precision.HIGHEST

D_MODEL = 1024
GRID_W = 64
NORM_EPS = 1e-6
SSM_D_INNER = 2048
SSM_HEADDIM = 64
SSM_HEADS = 32
SSM_GROUPS = 4
SSM_STATE = 128
SSM_CHUNK = 128
WKV_WIDTH = 1024
WKV_HEADSIZE = 64
WKV_HEADS = 16
WKV_RANK = 64
WKV_GN_EPS = 64e-5
WKV_CHUNK = 64
XBC_W = SSM_D_INNER + 2 * SSM_GROUPS * SSM_STATE
RWKV_SCAN_W = 3 * WKV_WIDTH + 2 * WKV_RANK
ZA_END = SSM_D_INNER
XBC_END = ZA_END + XBC_W
DT_END = XBC_END + SSM_HEADS
RW_END = DT_END + RWKV_SCAN_W
ZB_END = RW_END + WKV_WIDTH
IN_W = ZB_END + 2 * D_MODEL

LANES = 128
N_PAIRS = WKV_HEADS // 2
SSM_PAIRS = SSM_HEADS // 2
MAIN_W = 2 * SSM_D_INNER + XBC_W + WKV_WIDTH + LANES
VMEM_LIMIT = 48 * 1024 * 1024


def _cparams(sem):
    return pltpu.CompilerParams(dimension_semantics=sem, vmem_limit_bytes=VMEM_LIMIT)


def _sigmoid(x):
    return 1.0 / (1.0 + jnp.exp(-x))


def _silu(x):
    return x * _sigmoid(x)


def _softplus(x):
    return jnp.maximum(x, 0.0) + jnp.log(1.0 + jnp.exp(-jnp.abs(x)))


def _dot(a, b, precision=None):
    return jnp.dot(a, b, preferred_element_type=F32, precision=precision)


def _dot_nt(a, b, precision=None):
    return lax.dot_general(a, b, (((1,), (1,)), ((), ())), preferred_element_type=F32, precision=precision)


def _bdot(a, b):
    return _dot(a.astype(BF16), b.astype(BF16))


def _bdot_nt(a, b):
    return _dot_nt(a.astype(BF16), b.astype(BF16))


def _mod_kernel(c_ref, w_ref, b_ref, o_ref):
    c = c_ref[...]
    o_ref[...] = _dot(_silu(c), w_ref[...], HI) + b_ref[...]


def _modulation(cond8, w_mod, b_mod):
    n = w_mod.shape[1]
    tn = 1024
    return pl.pallas_call(
        _mod_kernel,
        grid=(n // tn,),
        in_specs=[pl.BlockSpec((8, D_MODEL), lambda j: (0, 0)),
                  pl.BlockSpec((D_MODEL, tn), lambda j: (0, j)),
                  pl.BlockSpec((1, tn), lambda j: (0, j))],
        out_specs=pl.BlockSpec((8, tn), lambda j: (0, j)),
        out_shape=jax.ShapeDtypeStruct((8, n), F32),
        compiler_params=_cparams(("arbitrary",)),
        name="modulation",
    )(cond8, w_mod, b_mod.reshape(1, n))


def _h_kernel(x_ref, mod_ref, g_ref, h_ref):
    x = x_ref[...]
    y = x * lax.rsqrt(jnp.mean(x * x, axis=-1, keepdims=True) + NORM_EPS) * g_ref[...]
    m = mod_ref[0]
    shift = m[:, :D_MODEL]
    scale = m[:, D_MODEL:2 * D_MODEL]
    h_ref[...] = (y * (1.0 + scale) + shift).astype(BF16)


def _adaln(x2, mod3, norm_g, row0, tiles_per_row, tm):
    t = x2.shape[0]
    return pl.pallas_call(
        _h_kernel,
        grid=(t // tm,),
        in_specs=[pl.BlockSpec((tm, D_MODEL), lambda i: (i, 0)),
                  pl.BlockSpec((1, 1, 3 * D_MODEL), lambda i: (row0 + i // tiles_per_row, 0, 0)),
                  pl.BlockSpec((1, D_MODEL), lambda i: (0, 0))],
        out_specs=pl.BlockSpec((tm, D_MODEL), lambda i: (i, 0)),
        out_shape=jax.ShapeDtypeStruct((t, D_MODEL), BF16),
        compiler_params=_cparams(("arbitrary",)),
        name="adaln_norm",
    )(x2, mod3, norm_g.reshape(1, D_MODEL))


def _proj_kernel(h_ref, w_ref, o_ref):
    o_ref[...] = _dot(h_ref[...], w_ref[...])


def _project(h, w, tn, tm=512):
    t, k = h.shape
    n = w.shape[1]
    return pl.pallas_call(
        _proj_kernel,
        grid=(n // tn, t // tm),
        in_specs=[pl.BlockSpec((tm, k), lambda j, i: (i, 0)),
                  pl.BlockSpec((k, tn), lambda j, i: (0, j))],
        out_specs=pl.BlockSpec((tm, tn), lambda j, i: (i, j)),
        out_shape=jax.ShapeDtypeStruct((t, n), F32),
        compiler_params=_cparams(("arbitrary", "arbitrary")),
        name="in_proj",
    )(h, w)


def _neighbours(x):
    n = x.shape[0]
    row = lax.broadcasted_iota(jnp.int32, (n, 1), 0)
    not_first = (row != 0).astype(F32)
    not_last = (row != n - 1).astype(F32)
    return pltpu.roll(x, 1, 0) * not_first, pltpu.roll(x, n - 1, 0) * not_last


def _conv_kernel(x_ref, w_ref, b_ref, o_ref):
    x = x_ref[0]
    prev, nxt = _neighbours(x)
    w = w_ref[...]
    y = w[0:1] * prev + w[1:2] * x + w[2:3] * nxt + b_ref[...]
    o_ref[0] = _silu(y)


def _conv_silu(p_main3, conv_w, conv_b):
    b, l, _ = p_main3.shape
    tc = 512
    off = (2 * SSM_D_INNER) // tc
    return pl.pallas_call(
        _conv_kernel,
        grid=(b, XBC_W // tc),
        in_specs=[pl.BlockSpec((1, l, tc), lambda i, j: (i, 0, off + j)),
                  pl.BlockSpec((3, tc), lambda i, j: (0, j)),
                  pl.BlockSpec((1, tc), lambda i, j: (0, j))],
        out_specs=pl.BlockSpec((1, l, tc), lambda i, j: (i, 0, j)),
        out_shape=jax.ShapeDtypeStruct((b, l, XBC_W), F32),
        compiler_params=_cparams(("arbitrary", "arbitrary")),
        name="conv_silu",
    )(p_main3, conv_w, conv_b.reshape(1, XBC_W))


def _shift_kernel(x_ref, mu_ref, o_ref):
    x = x_ref[0]
    prev, nxt = _neighbours(x)
    o_ref[0] = x + mu_ref[...] * (0.5 * (prev + nxt) - x)


def _token_shift(p_rw3, shift_mu):
    b, l, w = p_rw3.shape
    tc = 640
    return pl.pallas_call(
        _shift_kernel,
        grid=(b, w // tc),
        in_specs=[pl.BlockSpec((1, l, tc), lambda i, j: (i, 0, j)),
                  pl.BlockSpec((1, tc), lambda i, j: (0, j))],
        out_specs=pl.BlockSpec((1, l, tc), lambda i, j: (i, 0, j)),
        out_shape=jax.ShapeDtypeStruct((b, l, w), F32),
        compiler_params=_cparams(("arbitrary", "arbitrary")),
        name="token_shift",
    )(p_rw3, shift_mu.reshape(1, w))


def _ssd_kernel(*refs, rev, has_init, has_prev):
    refs = list(refs)
    xs_ref, b_ref, c_ref, dt_ref, alog_ref, dtb_ref = refs[:6]
    pos = 6
    s0_ref = None
    if has_init:
        s0_ref = refs[pos]
        pos += 1
    yprev_ref = dskip_ref = None
    if has_prev:
        yprev_ref, dskip_ref = refs[pos], refs[pos + 1]
        pos += 2
    y_ref, sfin_ref, s_ref = refs[pos], refs[pos + 1], refs[pos + 2]

    q = SSM_CHUNK
    c = pl.program_id(1)

    @pl.when(c == 0)
    def _():
        if has_init:
            s_ref[...] = s0_ref[0]
        else:
            s_ref[...] = jnp.zeros_like(s_ref)

    ii = lax.broadcasted_iota(jnp.int32, (q, q), 0)
    jj = lax.broadcasted_iota(jnp.int32, (q, q), 1)
    incl = (jj >= ii) if rev else (jj <= ii)
    lane_lo = jj < SSM_HEADDIM
    row_lo = ii < SSM_HEADDIM
    last = 0 if rev else q - 1

    dt = _softplus(dt_ref[0] + dtb_ref[...])
    a = dt * (-jnp.exp(alog_ref[...]))
    cs = _dot(incl.astype(F32), a, HI)
    cs_t = cs.T

    for g in range(SSM_GROUPS):
        bm = b_ref[0, :, g * SSM_STATE:(g + 1) * SSM_STATE].astype(BF16)
        cm = c_ref[0, :, g * SSM_STATE:(g + 1) * SSM_STATE].astype(BF16)
        cb = _dot_nt(cm, bm)
        for jp in range(SSM_PAIRS // SSM_GROUPS):
            p = g * (SSM_PAIRS // SSM_GROUPS) + jp
            h0, h1 = 2 * p, 2 * p + 1
            lanes = slice(p * LANES, (p + 1) * LANES)
            xs = xs_ref[0, :, lanes]
            cs_pair = jnp.where(lane_lo, cs[:, h0:h0 + 1], cs[:, h1:h1 + 1])
            dt_pair = jnp.where(lane_lo, dt[:, h0:h0 + 1], dt[:, h1:h1 + 1])
            xdt = xs * dt_pair
            xdt_b = xdt.astype(BF16)
            y_heads = []
            for h in (h0, h1):
                seg = cs[:, h:h + 1] - cs_t[h:h + 1, :]
                lm = jnp.where(incl, jnp.exp(jnp.minimum(seg, 0.0)), 0.0)
                y_heads.append(_dot((cb * lm).astype(BF16), xdt_b))
            y_diag = jnp.where(lane_lo, y_heads[0], y_heads[1])
            s_pair = s_ref[p]
            y_off = _dot_nt(cm, s_pair.astype(BF16)) * jnp.exp(cs_pair)
            y = y_diag + y_off
            if has_prev:
                y = y + yprev_ref[0, :, lanes] + dskip_ref[:, lanes] * xs
            y_ref[0, :, lanes] = y
            cs_end = cs_pair[last:last + 1, :]
            decay = jnp.exp(cs_end - cs_pair)
            upd = _dot((xdt * decay).T.astype(BF16), bm)
            end_col = jnp.where(row_lo, cs_t[h0:h0 + 1, last:last + 1], cs_t[h1:h1 + 1, last:last + 1])
            s_ref[p] = s_pair * jnp.exp(end_col) + upd

    @pl.when(c == pl.num_programs(1) - 1)
    def _():
        sfin_ref[0] = s_ref[...]


def _ssd_scan(xbc, p_main3, alog, dtb, s0, yprev, dskip, rev):
    b, l, _ = xbc.shape
    nc = l // SSM_CHUNK
    q = SSM_CHUNK
    cidx = (lambda c: nc - 1 - c) if rev else (lambda c: c)
    dt_blk = (MAIN_W - LANES) // LANES
    in_specs = [pl.BlockSpec((1, q, SSM_D_INNER), lambda i, c: (i, cidx(c), 0)),
                pl.BlockSpec((1, q, 512), lambda i, c: (i, cidx(c), SSM_D_INNER // 512)),
                pl.BlockSpec((1, q, 512), lambda i, c: (i, cidx(c), SSM_D_INNER // 512 + 1)),
                pl.BlockSpec((1, q, LANES), lambda i, c: (i, cidx(c), dt_blk)),
                pl.BlockSpec((1, LANES), lambda i, c: (0, 0)),
                pl.BlockSpec((1, LANES), lambda i, c: (0, 0))]
    args = [xbc, xbc, xbc, p_main3, alog, dtb]
    if s0 is not None:
        in_specs.append(pl.BlockSpec((1, SSM_PAIRS, LANES, SSM_STATE), lambda i, c: (i, 0, 0, 0)))
        args.append(s0)
    if yprev is not None:
        in_specs.append(pl.BlockSpec((1, q, SSM_D_INNER), lambda i, c: (i, cidx(c), 0)))
        in_specs.append(pl.BlockSpec((1, SSM_D_INNER), lambda i, c: (0, 0)))
        args += [yprev, dskip]
    kern = functools.partial(_ssd_kernel, rev=rev, has_init=s0 is not None, has_prev=yprev is not None)
    return pl.pallas_call(
        kern,
        grid=(b, nc),
        in_specs=in_specs,
        out_specs=[pl.BlockSpec((1, q, SSM_D_INNER), lambda i, c: (i, cidx(c), 0)),
                   pl.BlockSpec((1, SSM_PAIRS, LANES, SSM_STATE), lambda i, c: (i, 0, 0, 0))],
        out_shape=[jax.ShapeDtypeStruct((b, l, SSM_D_INNER), F32),
                   jax.ShapeDtypeStruct((b, SSM_PAIRS, LANES, SSM_STATE), F32)],
        scratch_shapes=[pltpu.VMEM((SSM_PAIRS, LANES, SSM_STATE), F32)],
        compiler_params=_cparams(("arbitrary", "arbitrary")),
        name="ssd_bwd" if rev else "ssd_fwd",
    )(*args)


def _block_diag(y, lane_lo):
    return jnp.concatenate([jnp.where(lane_lo, y, 0.0), jnp.where(lane_lo, 0.0, y)], axis=0)


def _wkv_kernel(*refs, rev, has_init, final):
    refs = list(refs)
    (rw_ref, w0_ref, w2_ref, a0_ref, a2_ref, kk_ref, ka_ref) = refs[:7]
    pos = 7
    s0_ref = None
    if has_init:
        s0_ref = refs[pos]
        pos += 1
    of_ref = rk_ref = lnw_ref = lnb_ref = None
    if final:
        of_ref, rk_ref, lnw_ref, lnb_ref = refs[pos:pos + 4]
        pos += 4
    o_ref, sfin_ref = refs[pos], refs[pos + 1]
    s_ref, r_s, k_s, v_s, a_s, lw_s, of_s, o_s = refs[pos + 2:pos + 10]

    n = WKV_CHUNK
    c = pl.program_id(1)

    @pl.when(c == 0)
    def _():
        if has_init:
            s_ref[...] = s0_ref[0]
        else:
            s_ref[...] = jnp.zeros_like(s_ref)

    xl = rw_ref[0, :, 3 * WKV_WIDTH:3 * WKV_WIDTH + LANES]
    a_full = _sigmoid(a0_ref[...] + _dot(xl, a2_ref[...], HI))
    xw = w0_ref[...] + _dot(jnp.tanh(xl), w2_ref[...], HI)
    lw_full = -math.exp(-0.5) * _sigmoid(xw)
    for p in range(N_PAIRS):
        lanes = slice(p * LANES, (p + 1) * LANES)
        r_s[p] = rw_ref[0, :, p * LANES:(p + 1) * LANES]
        k_s[p] = rw_ref[0, :, WKV_WIDTH + p * LANES:WKV_WIDTH + (p + 1) * LANES]
        v_s[p] = rw_ref[0, :, 2 * WKV_WIDTH + p * LANES:2 * WKV_WIDTH + (p + 1) * LANES]
        a_s[p] = a_full[:, lanes]
        lw_s[p] = lw_full[:, lanes]
        if final:
            of_s[p] = of_ref[0, :, lanes]

    ti = lax.broadcasted_iota(jnp.int32, (n, LANES), 0)
    li = lax.broadcasted_iota(jnp.int32, (n, LANES), 1)
    si = li & (WKV_HEADSIZE - 1)
    lane_lo = li < WKV_HEADSIZE
    strict = (si > ti) if rev else (si < ti)
    incl = (si >= ti) if rev else (si <= ti)
    eye = (si == ti).astype(F32)
    tq = lax.broadcasted_iota(jnp.int32, (n, n), 0)
    sq = lax.broadcasted_iota(jnp.int32, (n, n), 1)
    tri = ((sq >= tq) if rev else (sq <= tq)).astype(F32)
    r2 = lax.broadcasted_iota(jnp.int32, (LANES, LANES), 0)
    c2 = lax.broadcasted_iota(jnp.int32, (LANES, LANES), 1)
    same_head = (r2 < WKV_HEADSIZE) == (c2 < WKV_HEADSIZE)
    ones_bd = same_head.astype(F32)
    last = 0 if rev else n - 1

    def level_mask(m):
        same = (ti >> (m.bit_length())) == (si >> (m.bit_length()))
        t_hi = (ti & m) != 0
        s_hi = (si & m) != 0
        if rev:
            return same & jnp.logical_not(t_hi) & s_hi
        return same & t_hi & jnp.logical_not(s_hi)

    def pair_body(p, carry):
        r = r_s[p]
        k = k_s[p]
        v = v_s[p]
        a = a_s[p]
        lw = lw_s[p]
        k_k = kk_ref[p]
        k_a = ka_ref[p]
        kkr = k * k_k
        ss = _dot(kkr * kkr, ones_bd, HI)
        kk = kkr / jnp.maximum(jnp.sqrt(ss), 1e-12)
        kmod = k * (1.0 + (a - 1.0) * k_a)
        kka = kk * a

        lg = _dot(tri, lw, HI)
        g_inv = jnp.exp(-lg)
        rh = r * jnp.exp(lg)
        kh = kmod * g_inv
        ah = kka * g_inv
        bh = -kk * jnp.exp(lg - lw)
        lg_end = lg[last:last + 1, :]
        g_end = jnp.exp(lg_end - lg)
        a_e = kka * g_end
        k_e = kmod * g_end

        lhs = jnp.concatenate([bh, rh], axis=0)
        aa = _dot_nt(lhs, _block_diag(ah, lane_lo), HI)
        ak = _dot_nt(lhs, _block_diag(kh, lane_lo), HI)
        a_ab = jnp.where(strict, aa[:n], 0.0)
        a_ra = jnp.where(incl, aa[n:], 0.0)
        a_bk = jnp.where(strict, ak[:n], 0.0)
        a_rk = jnp.where(incl, ak[n:], 0.0)

        x = eye + jnp.where(level_mask(1), a_ab, 0.0)
        m = 2
        while m < n:
            e = jnp.where(level_mask(m), a_ab, 0.0)
            pm = _dot(x, _block_diag(e, lane_lo), HI)
            x = x + _dot(pm, _block_diag(x, lane_lo), HI)
            m *= 2

        s_bd = s_ref[p]
        x0r = _dot_nt(lhs, s_bd, HI)
        gv = _dot(jnp.concatenate([a_bk, a_rk], axis=0), _block_diag(v, lane_lo), HI)
        u = _dot(x, _block_diag(x0r[:n] + gv[:n], lane_lo), HI)
        o = x0r[n:] + gv[n:] + _dot(a_ra, _block_diag(u, lane_lo), HI)
        uv = jnp.concatenate([u, v], axis=0)
        ake = jnp.concatenate([a_e, k_e], axis=0)
        upd = _dot(uv.T, ake, HI)
        s_ref[p] = s_bd * jnp.exp(lg_end) + jnp.where(same_head, upd, 0.0)

        if final:
            o = o + of_s[p]
            mu = _dot(o, ones_bd, HI) * (1.0 / WKV_HEADSIZE)
            d = o - mu
            var = _dot(d * d, ones_bd, HI) * (1.0 / WKV_HEADSIZE)
            o = d * lax.rsqrt(var + WKV_GN_EPS) * lnw_ref[p] + lnb_ref[p]
            o = o + _dot(r * kmod * rk_ref[p], ones_bd, HI) * v
        o_s[p] = o
        return carry

    lax.fori_loop(0, N_PAIRS, pair_body, 0)

    for p in range(N_PAIRS):
        o_ref[0, :, p * LANES:(p + 1) * LANES] = o_s[p]

    @pl.when(c == pl.num_programs(1) - 1)
    def _():
        sfin_ref[0] = s_ref[...]


def _pairs(vec):
    return vec.reshape(N_PAIRS, 1, LANES)


def _wkv_scan(rw_s, w0, w2p, a0, a2p, k_k, k_a, s0, o_f, r_k, ln_w, ln_b, rev):
    b, l, w = rw_s.shape
    n = WKV_CHUNK
    nc = l // n
    cidx = (lambda c: nc - 1 - c) if rev else (lambda c: c)
    final = o_f is not None
    vec_spec = pl.BlockSpec((N_PAIRS, 1, LANES), lambda i, c: (0, 0, 0))
    st_spec = pl.BlockSpec((1, N_PAIRS, LANES, LANES), lambda i, c: (i, 0, 0, 0))
    in_specs = [pl.BlockSpec((1, n, w), lambda i, c: (i, cidx(c), 0)),
                pl.BlockSpec((1, WKV_WIDTH), lambda i, c: (0, 0)),
                pl.BlockSpec((LANES, WKV_WIDTH), lambda i, c: (0, 0)),
                pl.BlockSpec((1, WKV_WIDTH), lambda i, c: (0, 0)),
                pl.BlockSpec((LANES, WKV_WIDTH), lambda i, c: (0, 0)),
                vec_spec, vec_spec]
    args = [rw_s, w0.reshape(1, WKV_WIDTH), w2p, a0.reshape(1, WKV_WIDTH), a2p, _pairs(k_k), _pairs(k_a)]
    if s0 is not None:
        in_specs.append(st_spec)
        args.append(s0)
    if final:
        in_specs += [pl.BlockSpec((1, n, WKV_WIDTH), lambda i, c: (i, cidx(c), 0)), vec_spec, vec_spec, vec_spec]
        args += [o_f, _pairs(r_k), _pairs(ln_w), _pairs(ln_b)]
    kern = functools.partial(_wkv_kernel, rev=rev, has_init=s0 is not None, final=final)
    pair_buf = pltpu.VMEM((N_PAIRS, n, LANES), F32)
    return pl.pallas_call(
        kern,
        grid=(b, nc),
        in_specs=in_specs,
        out_specs=[pl.BlockSpec((1, n, WKV_WIDTH), lambda i, c: (i, cidx(c), 0)), st_spec],
        out_shape=[jax.ShapeDtypeStruct((b, l, WKV_WIDTH), F32),
                   jax.ShapeDtypeStruct((b, N_PAIRS, LANES, LANES), F32)],
        scratch_shapes=[pltpu.VMEM((N_PAIRS, LANES, LANES), F32)] + [pair_buf] * 7,
        compiler_params=_cparams(("arbitrary", "arbitrary")),
        name="wkv_bwd" if rev else "wkv_fwd",
    )(*args)


def _final_kernel(y_ref, za_ref, o_ref, zb_ref, ga_ref, gb_ref, x_ref, mod_ref,
                  ng_ref, pa_ref, pb_ref, wo_ref, fg_ref, out_ref):
    y = y_ref[...] * _silu(za_ref[...])
    y = y * lax.rsqrt(jnp.mean(y * y, axis=-1, keepdims=True) + NORM_EPS) * ng_ref[...]
    u_a = _dot(y.astype(BF16), pa_ref[...])
    u_b = _dot((o_ref[...] * _silu(zb_ref[...])).astype(BF16), pb_ref[...])
    m = _sigmoid(ga_ref[...]) * u_a + _sigmoid(gb_ref[...]) * u_b
    out = _dot(m.astype(BF16), wo_ref[...])
    gate = mod_ref[0][:, 2 * D_MODEL:]
    xo = x_ref[...] + gate * out
    out_ref[...] = xo * lax.rsqrt(jnp.mean(xo * xo, axis=-1, keepdims=True) + NORM_EPS) * fg_ref[...]


def _final(y2, p_main, o2, x2, mod3, ssm_norm_g, p_a, p_b, w_out, final_g, row0, tiles_per_row, tm):
    t = x2.shape[0]
    const = lambda i: (0, 0)
    w1 = D_MODEL
    return pl.pallas_call(
        _final_kernel,
        grid=(t // tm,),
        in_specs=[pl.BlockSpec((tm, SSM_D_INNER), lambda i: (i, 0)),
                  pl.BlockSpec((tm, SSM_D_INNER), lambda i: (i, 0)),
                  pl.BlockSpec((tm, w1), lambda i: (i, 0)),
                  pl.BlockSpec((tm, w1), lambda i: (i, (2 * SSM_D_INNER + XBC_W) // w1)),
                  pl.BlockSpec((tm, w1), lambda i: (i, SSM_D_INNER // w1)),
                  pl.BlockSpec((tm, w1), lambda i: (i, SSM_D_INNER // w1 + 1)),
                  pl.BlockSpec((tm, w1), lambda i: (i, 0)),
                  pl.BlockSpec((1, 1, 3 * D_MODEL), lambda i: (row0 + i // tiles_per_row, 0, 0)),
                  pl.BlockSpec((1, SSM_D_INNER), const),
                  pl.BlockSpec((SSM_D_INNER, D_MODEL), const),
                  pl.BlockSpec((WKV_WIDTH, D_MODEL), const),
                  pl.BlockSpec((D_MODEL, D_MODEL), const),
                  pl.BlockSpec((1, D_MODEL), const)],
        out_specs=pl.BlockSpec((tm, D_MODEL), lambda i: (i, 0)),
        out_shape=jax.ShapeDtypeStruct((t, D_MODEL), F32),
        compiler_params=_cparams(("arbitrary",)),
        name="merge_out",
    )(y2, p_main, o2, p_main, p_main, p_main, x2, mod3,
      ssm_norm_g.reshape(1, SSM_D_INNER), p_a, p_b, w_out, final_g.reshape(1, D_MODEL))


def _to_col_major(x):
    b, l, c = x.shape
    rows = l // GRID_W
    return x.reshape(b, rows, GRID_W, c).transpose(0, 2, 1, 3).reshape(b, l, c)


def _from_col_major(x):
    b, l, c = x.shape
    rows = l // GRID_W
    return x.reshape(b, GRID_W, rows, c).transpose(0, 2, 1, 3).reshape(b, l, c)


def _wkv_state_to_pairs(s):
    b = s.shape[0]
    s = s.reshape(b, N_PAIRS, 2, WKV_HEADSIZE, WKV_HEADSIZE)
    z = jnp.zeros_like(s[:, :, 0])
    top = jnp.concatenate([s[:, :, 0], z], axis=-1)
    bot = jnp.concatenate([z, s[:, :, 1]], axis=-1)
    return jnp.concatenate([top, bot], axis=-2)


def _wkv_state_from_pairs(s):
    b = s.shape[0]
    h = WKV_HEADSIZE
    return jnp.stack([s[:, :, :h, :h], s[:, :, h:, h:]], axis=2).reshape(b, WKV_HEADS, h, h)


def _pad_lanes(v):
    return jnp.pad(v, (0, LANES - v.shape[0])).reshape(1, LANES)


def _group(x, mod3, row0, grid, states, wts):
    b, l, _ = x.shape
    t = b * l
    tm = 256
    x2 = x.reshape(t, D_MODEL)
    rows_per_cond = l // tm if grid else t // tm
    h = _adaln(x2, mod3, wts["norm_g"], row0, rows_per_cond, tm)
    p_main = _project(h, wts["w_main"], tn=1664)
    h_rw = _to_col_major(h.reshape(b, l, D_MODEL)).reshape(t, D_MODEL) if grid else h
    p_rw = _project(h_rw, wts["w_rw"], tn=640)
    p_main3 = p_main.reshape(b, l, MAIN_W)

    xbc = _conv_silu(p_main3, wts["conv_w"], wts["conv_b"])
    s_f = s_b = None
    if states is not None:
        s_f = states[0].reshape(b, SSM_PAIRS, LANES, SSM_STATE)
        s_b = states[1].reshape(b, SSM_PAIRS, LANES, SSM_STATE)
    y_f, fs_f = _ssd_scan(xbc, p_main3, wts["alog"][0], wts["dtb"][0], s_f, None, None, rev=False)
    y, fs_b = _ssd_scan(xbc, p_main3, wts["alog"][1], wts["dtb"][1], s_b, y_f, wts["dskip"], rev=True)

    rw_s = _token_shift(p_rw.reshape(b, l, RWKV_SCAN_W), wts["shift_mu"])
    w_f = w_b = None
    if states is not None:
        w_f = _wkv_state_to_pairs(states[2])
        w_b = _wkv_state_to_pairs(states[3])
    o_f, fw_f = _wkv_scan(rw_s, wts["w0"][0], wts["w2p"][0], wts["a0"], wts["a2p"], wts["k_k"], wts["k_a"],
                          w_f, None, None, None, None, rev=False)
    o, fw_b = _wkv_scan(rw_s, wts["w0"][1], wts["w2p"][1], wts["a0"], wts["a2p"], wts["k_k"], wts["k_a"],
                        w_b, o_f, wts["r_k"], wts["ln_w"], wts["ln_b"], rev=True)
    if grid:
        o = _from_col_major(o)

    out = _final(y.reshape(t, SSM_D_INNER), p_main, o.reshape(t, WKV_WIDTH), x2, mod3, wts["ssm_norm_g"],
                 wts["p_a"], wts["p_b"], wts["w_out"], wts["final_g"], row0, rows_per_cond, tm)
    finals = (fs_f.reshape(b, SSM_HEADS, SSM_HEADDIM, SSM_STATE), fs_b.reshape(b, SSM_HEADS, SSM_HEADDIM, SSM_STATE),
              _wkv_state_from_pairs(fw_f), _wkv_state_from_pairs(fw_b))
    return out.reshape(b, l, D_MODEL), finals


def kernel(x_prompt, x_sample, state_ssm_fwd, state_ssm_bwd, state_wkv_fwd, state_wkv_bwd, c, c_ctx, w_mod, b_mod, norm_g, w_in, conv_w, conv_b, a_log, dt_bias, d_skip, ssm_norm_g, p_a, shift_mu, w0, w2, a0, a2, k_k, k_a, r_k, ln_w, ln_b, p_b, w_out, final_g):
    depth = w_mod.shape[0]
    assert depth == 1, "single-layer stack only"
    l0 = 0
    w_in0 = w_in[l0]
    zpad = jnp.zeros((WKV_RANK, WKV_WIDTH), F32)
    w_main = jnp.concatenate([w_in0[:, :ZA_END], w_in0[:, ZB_END:], w_in0[:, ZA_END:XBC_END], w_in0[:, RW_END:ZB_END],
                              w_in0[:, XBC_END:DT_END], jnp.zeros((D_MODEL, LANES - SSM_HEADS), F32)], axis=1)
    wts = {
        "norm_g": norm_g[l0],
        "w_main": w_main.astype(BF16),
        "w_rw": w_in0[:, DT_END:RW_END].astype(BF16),
        "conv_w": conv_w[l0], "conv_b": conv_b[l0],
        "alog": [_pad_lanes(a_log[l0, d]) for d in range(2)],
        "dtb": [_pad_lanes(dt_bias[l0, d]) for d in range(2)],
        "dskip": jnp.repeat(d_skip[l0], SSM_HEADDIM).reshape(1, SSM_D_INNER),
        "ssm_norm_g": ssm_norm_g[l0],
        "p_a": p_a[l0].astype(BF16), "p_b": p_b[l0].astype(BF16), "w_out": w_out[l0].astype(BF16),
        "shift_mu": shift_mu[l0],
        "w0": w0[l0],
        "w2p": [jnp.concatenate([w2[l0, d], zpad], axis=0) for d in range(2)],
        "a0": a0[l0],
        "a2p": jnp.concatenate([zpad, a2[l0]], axis=0),
        "k_k": k_k[l0], "k_a": k_a[l0], "r_k": r_k[l0], "ln_w": ln_w[l0], "ln_b": ln_b[l0],
        "final_g": final_g,
    }
    nb = c.shape[0]
    cond8 = jnp.concatenate([c_ctx[None, :], c, jnp.zeros((8 - 1 - nb, D_MODEL), F32)], axis=0)
    mod3 = _modulation(cond8, w_mod[l0], b_mod[l0]).reshape(8, 1, 3 * D_MODEL)

    y_prompt, (sf, sb, wf, wb) = _group(x_prompt, mod3, 0, False, None, wts)
    lat_states = (state_ssm_fwd[:, l0], state_ssm_bwd[:, l0], state_wkv_fwd[:, l0], state_wkv_bwd[:, l0])
    y_sample, _ = _group(x_sample, mod3, 1, True, lat_states, wts)
    return (y_prompt, y_sample, sf[:, None], sb[:, None], wf[:, None], wb[:, None])
```

```python
import functools
import math

import jax
import jax.numpy as jnp
from jax import lax
from jax.experimental import pallas as pl
from jax.experimental.pallas import tpu as pltpu

F32 = jnp.float32
BF16 = jnp.bfloat16
HI = lax.Precision.HIGHEST

D_MODEL = 1024
GRID_W = 64
NORM_EPS = 1e-6
SSM_D_INNER = 2048
SSM_HEADDIM = 64
SSM_HEADS = 32
SSM_GROUPS = 4
SSM_STATE = 128
SSM_CHUNK = 128
WKV_WIDTH = 1024
WKV_HEADSIZE = 64
WKV_HEADS = 16
WKV_RANK = 64
WKV_GN_EPS = 64e-5
WKV_CHUNK = 64
XBC_W = SSM_D_INNER + 2 * SSM_GROUPS * SSM_STATE
RWKV_SCAN_W = 3 * WKV_WIDTH + 2 * WKV_RANK
ZA_END = SSM_D_INNER
XBC_END = ZA_END + XBC_W
DT_END = XBC_END + SSM_HEADS
RW_END = DT_END + RWKV_SCAN_W
ZB_END = RW_END + WKV_WIDTH
IN_W = ZB_END + 2 * D_MODEL

LANES = 128
N_PAIRS = WKV_HEADS // 2
SSM_PAIRS = SSM_HEADS // 2
WKV_LOCKSTEP = 8
MAIN_W = 2 * SSM_D_INNER + XBC_W + WKV_WIDTH + LANES
VMEM_LIMIT = 48 * 1024 * 1024


def _cparams(sem):
    return pltpu.CompilerParams(dimension_semantics=sem, vmem_limit_bytes=VMEM_LIMIT)


def _sigmoid(x):
    return 1.0 / (1.0 + jnp.exp(-x))


def _silu(x):
    return x * _sigmoid(x)


def _softplus(x):
    return jnp.maximum(x, 0.0) + jnp.log(1.0 + jnp.exp(-jnp.abs(x)))


def _dot(a, b, precision=None):
    return jnp.dot(a, b, preferred_element_type=F32, precision=precision)


def _dot_nt(a, b, precision=None):
    return lax.dot_general(a, b, (((1,), (1,)), ((), ())), preferred_element_type=F32, precision=precision)


def _bdot(a, b):
    return _dot(a.astype(BF16), b.astype(BF16))


def _bdot_nt(a, b):
    return _dot_nt(a.astype(BF16), b.astype(BF16))


def _mod_kernel(c_ref, w_ref, b_ref, o_ref):
    c = c_ref[...]
    o_ref[...] = _dot(_silu(c), w_ref[...], HI) + b_ref[...]


def _modulation(cond8, w_mod, b_mod):
    n = w_mod.shape[1]
    tn = 1024
    return pl.pallas_call(
        _mod_kernel,
        grid=(n // tn,),
        in_specs=[pl.BlockSpec((8, D_MODEL), lambda j: (0, 0)),
                  pl.BlockSpec((D_MODEL, tn), lambda j: (0, j)),
                  pl.BlockSpec((1, tn), lambda j: (0, j))],
        out_specs=pl.BlockSpec((8, tn), lambda j: (0, j)),
        out_shape=jax.ShapeDtypeStruct((8, n), F32),
        compiler_params=_cparams(("arbitrary",)),
        name="modulation",
    )(cond8, w_mod, b_mod.reshape(1, n))


def _h_kernel(x_ref, mod_ref, g_ref, h_ref):
    x = x_ref[...]
    y = x * lax.rsqrt(jnp.mean(x * x, axis=-1, keepdims=True) + NORM_EPS) * g_ref[...]
    m = mod_ref[0]
    shift = m[:, :D_MODEL]
    scale = m[:, D_MODEL:2 * D_MODEL]
    h_ref[...] = (y * (1.0 + scale) + shift).astype(BF16)


def _adaln(x2, mod3, norm_g, row0, tiles_per_row, tm):
    t = x2.shape[0]
    return pl.pallas_call(
        _h_kernel,
        grid=(t // tm,),
        in_specs=[pl.BlockSpec((tm, D_MODEL), lambda i: (i, 0)),
                  pl.BlockSpec((1, 1, 3 * D_MODEL), lambda i: (row0 + i // tiles_per_row, 0, 0)),
                  pl.BlockSpec((1, D_MODEL), lambda i: (0, 0))],
        out_specs=pl.BlockSpec((tm, D_MODEL), lambda i: (i, 0)),
        out_shape=jax.ShapeDtypeStruct((t, D_MODEL), BF16),
        compiler_params=_cparams(("arbitrary",)),
        name="adaln_norm",
    )(x2, mod3, norm_g.reshape(1, D_MODEL))


def _proj_kernel(h_ref, w_ref, o_ref):
    o_ref[...] = _dot(h_ref[...], w_ref[...])


def _project(h, w, tn, tm=512):
    t, k = h.shape
    n = w.shape[1]
    return pl.pallas_call(
        _proj_kernel,
        grid=(n // tn, t // tm),
        in_specs=[pl.BlockSpec((tm, k), lambda j, i: (i, 0)),
                  pl.BlockSpec((k, tn), lambda j, i: (0, j))],
        out_specs=pl.BlockSpec((tm, tn), lambda j, i: (i, j)),
        out_shape=jax.ShapeDtypeStruct((t, n), F32),
        compiler_params=_cparams(("arbitrary", "arbitrary")),
        name="in_proj",
    )(h, w)


def _neighbours(x):
    n = x.shape[0]
    row = lax.broadcasted_iota(jnp.int32, (n, 1), 0)
    not_first = (row != 0).astype(F32)
    not_last = (row != n - 1).astype(F32)
    return pltpu.roll(x, 1, 0) * not_first, pltpu.roll(x, n - 1, 0) * not_last


def _conv_kernel(x_ref, w_ref, b_ref, o_ref):
    x = x_ref[0]
    prev, nxt = _neighbours(x)
    w = w_ref[...]
    y = w[0:1] * prev + w[1:2] * x + w[2:3] * nxt + b_ref[...]
    o_ref[0] = _silu(y)


def _conv_silu(p_main3, conv_w, conv_b):
    b, l, _ = p_main3.shape
    tc = 512
    off = (2 * SSM_D_INNER) // tc
    return pl.pallas_call(
        _conv_kernel,
        grid=(b, XBC_W // tc),
        in_specs=[pl.BlockSpec((1, l, tc), lambda i, j: (i, 0, off + j)),
                  pl.BlockSpec((3, tc), lambda i, j: (0, j)),
                  pl.BlockSpec((1, tc), lambda i, j: (0, j))],
        out_specs=pl.BlockSpec((1, l, tc), lambda i, j: (i, 0, j)),
        out_shape=jax.ShapeDtypeStruct((b, l, XBC_W), F32),
        compiler_params=_cparams(("arbitrary", "arbitrary")),
        name="conv_silu",
    )(p_main3, conv_w, conv_b.reshape(1, XBC_W))


def _shift_kernel(x_ref, mu_ref, o_ref):
    x = x_ref[0]
    prev, nxt = _neighbours(x)
    o_ref[0] = x + mu_ref[...] * (0.5 * (prev + nxt) - x)


def _token_shift(p_rw3, shift_mu):
    b, l, w = p_rw3.shape
    tc = 640
    return pl.pallas_call(
        _shift_kernel,
        grid=(b, w // tc),
        in_specs=[pl.BlockSpec((1, l, tc), lambda i, j: (i, 0, j)),
                  pl.BlockSpec((1, tc), lambda i, j: (0, j))],
        out_specs=pl.BlockSpec((1, l, tc), lambda i, j: (i, 0, j)),
        out_shape=jax.ShapeDtypeStruct((b, l, w), F32),
        compiler_params=_cparams(("arbitrary", "arbitrary")),
        name="token_shift",
    )(p_rw3, shift_mu.reshape(1, w))


def _ssd_kernel(*refs, rev, has_init, has_prev):
    refs = list(refs)
    xs_ref, b_ref, c_ref, dt_ref, alog_ref, dtb_ref = refs[:6]
    pos = 6
    s0_ref = None
    if has_init:
        s0_ref = refs[pos]
        pos += 1
    yprev_ref = dskip_ref = None
    if has_prev:
        yprev_ref, dskip_ref = refs[pos], refs[pos + 1]
        pos += 2
    y_ref, sfin_ref, s_ref = refs[pos], refs[pos + 1], refs[pos + 2]

    q = SSM_CHUNK
    c = pl.program_id(1)

    @pl.when(c == 0)
    def _():
        if has_init:
            s_ref[...] = s0_ref[0]
        else:
            s_ref[...] = jnp.zeros_like(s_ref)

    ii = lax.broadcasted_iota(jnp.int32, (q, q), 0)
    jj = lax.broadcasted_iota(jnp.int32, (q, q), 1)
    incl = (jj >= ii) if rev else (jj <= ii)
    lane_lo = jj < SSM_HEADDIM
    row_lo = ii < SSM_HEADDIM
    last = 0 if rev else q - 1

    dt = _softplus(dt_ref[0] + dtb_ref[...])
    a = dt * (-jnp.exp(alog_ref[...]))
    cs = _dot(incl.astype(F32), a, HI)
    cs_t = cs.T

    for g in range(SSM_GROUPS):
        bm = b_ref[0, :, g * SSM_STATE:(g + 1) * SSM_STATE].astype(BF16)
        cm = c_ref[0, :, g * SSM_STATE:(g + 1) * SSM_STATE].astype(BF16)
        cb = _dot_nt(cm, bm)
        for jp in range(SSM_PAIRS // SSM_GROUPS):
            p = g * (SSM_PAIRS // SSM_GROUPS) + jp
            h0, h1 = 2 * p, 2 * p + 1
            lanes = slice(p * LANES, (p + 1) * LANES)
            xs = xs_ref[0, :, lanes]
            cs_pair = jnp.where(lane_lo, cs[:, h0:h0 + 1], cs[:, h1:h1 + 1])
            dt_pair = jnp.where(lane_lo, dt[:, h0:h0 + 1], dt[:, h1:h1 + 1])
            xdt = xs * dt_pair
            xdt_b = xdt.astype(BF16)
            y_heads = []
            for h in (h0, h1):
                seg = cs[:, h:h + 1] - cs_t[h:h + 1, :]
                lm = jnp.where(incl, jnp.exp(jnp.minimum(seg, 0.0)), 0.0)
                y_heads.append(_dot((cb * lm).astype(BF16), xdt_b))
            y_diag = jnp.where(lane_lo, y_heads[0], y_heads[1])
            s_pair = s_ref[p]
            y_off = _dot_nt(cm, s_pair.astype(BF16)) * jnp.exp(cs_pair)
            y = y_diag + y_off
            if has_prev:
                y = y + yprev_ref[0, :, lanes] + dskip_ref[:, lanes] * xs
            y_ref[0, :, lanes] = y
            cs_end = cs_pair[last:last + 1, :]
            decay = jnp.exp(cs_end - cs_pair)
            upd = _dot((xdt * decay).T.astype(BF16), bm)
            end_col = jnp.where(row_lo, cs_t[h0:h0 + 1, last:last + 1], cs_t[h1:h1 + 1, last:last + 1])
            s_ref[p] = s_pair * jnp.exp(end_col) + upd

    @pl.when(c == pl.num_programs(1) - 1)
    def _():
        sfin_ref[0] = s_ref[...]


def _ssd_scan(xbc, p_main3, alog, dtb, s0, yprev, dskip, rev):
    b, l, _ = xbc.shape
    nc = l // SSM_CHUNK
    q = SSM_CHUNK
    cidx = (lambda c: nc - 1 - c) if rev else (lambda c: c)
    dt_blk = (MAIN_W - LANES) // LANES
    in_specs = [pl.BlockSpec((1, q, SSM_D_INNER), lambda i, c: (i, cidx(c), 0)),
                pl.BlockSpec((1, q, 512), lambda i, c: (i, cidx(c), SSM_D_INNER // 512)),
                pl.BlockSpec((1, q, 512), lambda i, c: (i, cidx(c), SSM_D_INNER // 512 + 1)),
                pl.BlockSpec((1, q, LANES), lambda i, c: (i, cidx(c), dt_blk)),
                pl.BlockSpec((1, LANES), lambda i, c: (0, 0)),
                pl.BlockSpec((1, LANES), lambda i, c: (0, 0))]
    args = [xbc, xbc, xbc, p_main3, alog, dtb]
    if s0 is not None:
        in_specs.append(pl.BlockSpec((1, SSM_PAIRS, LANES, SSM_STATE), lambda i, c: (i, 0, 0, 0)))
        args.append(s0)
    if yprev is not None:
        in_specs.append(pl.BlockSpec((1, q, SSM_D_INNER), lambda i, c: (i, cidx(c), 0)))
        in_specs.append(pl.BlockSpec((1, SSM_D_INNER), lambda i, c: (0, 0)))
        args += [yprev, dskip]
    kern = functools.partial(_ssd_kernel, rev=rev, has_init=s0 is not None, has_prev=yprev is not None)
    return pl.pallas_call(
        kern,
        grid=(b, nc),
        in_specs=in_specs,
        out_specs=[pl.BlockSpec((1, q, SSM_D_INNER), lambda i, c: (i, cidx(c), 0)),
                   pl.BlockSpec((1, SSM_PAIRS, LANES, SSM_STATE), lambda i, c: (i, 0, 0, 0))],
        out_shape=[jax.ShapeDtypeStruct((b, l, SSM_D_INNER), F32),
                   jax.ShapeDtypeStruct((b, SSM_PAIRS, LANES, SSM_STATE), F32)],
        scratch_shapes=[pltpu.VMEM((SSM_PAIRS, LANES, SSM_STATE), F32)],
        compiler_params=_cparams(("arbitrary", "arbitrary")),
        name="ssd_bwd" if rev else "ssd_fwd",
    )(*args)


PREC = {
    "lora": (1, 1),
    "cum": (1, 2),
    "seg": (2, 1),
    "amat": (1, 1),
    "inv": (1, 1),
    "state": (1, 1),
    "gv": (1, 1),
    "u": (1, 1),
    "o": (1, 1),
    "upd": (1, 1),
}


def _pieces(x, n):
    out = []
    for i in range(n):
        h = x.astype(BF16)
        out.append(h)
        if i + 1 < n:
            x = x - h.astype(F32)
    return out


def _mm(a_pieces, b_pieces, nt=False):
    dot = _dot_nt if nt else _dot
    depth = max(len(a_pieces), len(b_pieces))
    acc = None
    for i in reversed(range(len(a_pieces))):
        for j in reversed(range(len(b_pieces))):
            if i + j < depth:
                t = dot(a_pieces[i], b_pieces[j])
                acc = t if acc is None else acc + t
    return acc


def _block_diag(y, m_lo, m_hi):
    return jnp.concatenate([y * m_lo, y * m_hi], axis=0)


def _wkv_kernel(*refs, rev, has_init, final):
    refs = list(refs)
    (rw_ref, w0_ref, w2_ref, a0_ref, a2_ref, kk_ref, ka_ref) = refs[:7]
    pos = 7
    s0_ref = None
    if has_init:
        s0_ref = refs[pos]
        pos += 1
    of_ref = rk_ref = lnw_ref = lnb_ref = None
    if final:
        of_ref, rk_ref, lnw_ref, lnb_ref = refs[pos:pos + 4]
        pos += 4
    o_ref, sfin_ref, s_ref = refs[pos], refs[pos + 1], refs[pos + 2]

    n = WKV_CHUNK
    c = pl.program_id(1)

    @pl.when(c == 0)
    def _():
        if has_init:
            s_ref[...] = s0_ref[0]
        else:
            s_ref[...] = jnp.zeros_like(s_ref)

    xl = rw_ref[0, :, 3 * WKV_WIDTH:3 * WKV_WIDTH + LANES]
    a_full = _sigmoid(a0_ref[...] + _mm(_pieces(xl, PREC["lora"][0]), [a2_ref[...]]))
    xw = w0_ref[...] + _mm(_pieces(jnp.tanh(xl), PREC["lora"][0]), [w2_ref[...]])
    lw_full = -math.exp(-0.5) * _sigmoid(xw)

    ti = lax.broadcasted_iota(jnp.int32, (n, LANES), 0)
    li = lax.broadcasted_iota(jnp.int32, (n, LANES), 1)
    si = li & (WKV_HEADSIZE - 1)
    strict = (si > ti) if rev else (si < ti)
    incl = (si >= ti) if rev else (si <= ti)
    eye = (si == ti).astype(F32)
    m_lo = (li < WKV_HEADSIZE).astype(BF16)
    m_hi = (li >= WKV_HEADSIZE).astype(BF16)
    tq = lax.broadcasted_iota(jnp.int32, (n, n), 0)
    sq = lax.broadcasted_iota(jnp.int32, (n, n), 1)
    tri = [((sq >= tq) if rev else (sq <= tq)).astype(BF16)]
    r2 = lax.broadcasted_iota(jnp.int32, (LANES, LANES), 0)
    c2 = lax.broadcasted_iota(jnp.int32, (LANES, LANES), 1)
    same_head = (r2 < WKV_HEADSIZE) == (c2 < WKV_HEADSIZE)
    ones_bd = [same_head.astype(BF16)]
    last = 0 if rev else n - 1

    def level_mask(m):
        same = (ti >> (m.bit_length())) == (si >> (m.bit_length()))
        t_hi = (ti & m) != 0
        s_hi = (si & m) != 0
        if rev:
            return same & jnp.logical_not(t_hi) & s_hi
        return same & t_hi & jnp.logical_not(s_hi)

    levels = []
    m = 2
    while m < n:
        levels.append(level_mask(m).astype(BF16))
        m *= 2
    level1 = level_mask(1)

    def bd(pieces):
        return [_block_diag(q, m_lo, m_hi) for q in pieces]

    def seg_sum(x):
        return _mm(_pieces(x, PREC["seg"][0]), ones_bd)

    def lane_blk(p, base=0):
        return slice(base + p * LANES, base + (p + 1) * LANES)

    for g0 in range(0, N_PAIRS, WKV_LOCKSTEP):
        ps = list(range(g0, g0 + WKV_LOCKSTEP))
        idx = range(len(ps))
        r = [rw_ref[0, :, lane_blk(p)] for p in ps]
        k = [rw_ref[0, :, lane_blk(p, WKV_WIDTH)] for p in ps]
        v = [rw_ref[0, :, lane_blk(p, 2 * WKV_WIDTH)] for p in ps]
        a = [a_full[:, lane_blk(p)] for p in ps]
        lw = [lw_full[:, lane_blk(p)] for p in ps]
        kkr = [k[i] * kk_ref[p] for i, p in enumerate(ps)]
        ss = [seg_sum(q * q) for q in kkr]
        kk = [kkr[i] / jnp.maximum(jnp.sqrt(ss[i]), 1e-12) for i in idx]
        kmod = [k[i] * (1.0 + (a[i] - 1.0) * ka_ref[p]) for i, p in enumerate(ps)]
        kka = [kk[i] * a[i] for i in idx]

        lg = [_mm(tri, _pieces(q, PREC["cum"][1])) for q in lw]
        g_inv = [jnp.exp(-q) for q in lg]
        rh = [r[i] * jnp.exp(lg[i]) for i in idx]
        kh = [kmod[i] * g_inv[i] for i in idx]
        ah = [kka[i] * g_inv[i] for i in idx]
        bh = [-kk[i] * jnp.exp(lg[i] - lw[i]) for i in idx]
        g_end = [jnp.exp(q[last:last + 1, :]) for q in lg]

        pa, pb = PREC["amat"]
        lhs = [_pieces(jnp.concatenate([bh[i], rh[i]], axis=0), max(pa, PREC["state"][0])) for i in idx]
        aa = [_mm(lhs[i][:pa], bd(_pieces(ah[i], pb)), nt=True) for i in idx]
        ak = [_mm(lhs[i][:pa], bd(_pieces(kh[i], pb)), nt=True) for i in idx]
        a_ab = [jnp.where(strict, q[:n], 0.0) for q in aa]
        a_ra = [jnp.where(incl, q[n:], 0.0) for q in aa]
        a_bk = [jnp.where(strict, q[:n], 0.0) for q in ak]
        a_rk = [jnp.where(incl, q[n:], 0.0) for q in ak]

        pa, pb = PREC["inv"]
        ab_pieces = [_pieces(q, pb) for q in a_ab]
        x = [eye + jnp.where(level1, q, 0.0) for q in a_ab]
        for lm in levels:
            e = [bd([q * lm for q in ab_pieces[i]]) for i in idx]
            pm = [_mm(_pieces(x[i], pa), e[i]) for i in idx]
            x = [x[i] + _mm(_pieces(pm[i], pa), bd(_pieces(x[i], pb))) for i in idx]

        s_bd = [s_ref[p] for p in ps]
        x0r = [_mm(lhs[i][:PREC["state"][0]], _pieces(s_bd[i], PREC["state"][1]), nt=True) for i in idx]
        pa, pb = PREC["gv"]
        gv = [_mm(_pieces(jnp.concatenate([a_bk[i], a_rk[i]], axis=0), pa), bd(_pieces(v[i], pb))) for i in idx]
        pa, pb = PREC["u"]
        u = [_mm(_pieces(x[i], pa), bd(_pieces(x0r[i][:n] + gv[i][:n], pb))) for i in idx]
        pa, pb = PREC["o"]
        o = [x0r[i][n:] + gv[i][n:] + _mm(_pieces(a_ra[i], pa), bd(_pieces(u[i], pb))) for i in idx]
        pa, pb = PREC["upd"]
        uvt = [jnp.concatenate([u[i], v[i]], axis=0).T for i in idx]
        ake = [jnp.concatenate([ah[i], kh[i]], axis=0) * g_end[i] for i in idx]
        upd = [_mm(_pieces(uvt[i], pa), _pieces(ake[i], pb)) for i in idx]
        for i, p in enumerate(ps):
            s_ref[p] = s_bd[i] * g_end[i] + jnp.where(same_head, upd[i], 0.0)

        if final:
            o = [o[i] + of_ref[0, :, lane_blk(p)] for i, p in enumerate(ps)]
            mu = [seg_sum(q) * (1.0 / WKV_HEADSIZE) for q in o]
            d = [o[i] - mu[i] for i in idx]
            var = [seg_sum(q * q) * (1.0 / WKV_HEADSIZE) for q in d]
            bonus = [seg_sum(r[i] * kmod[i] * rk_ref[p]) * v[i] for i, p in enumerate(ps)]
            o = [d[i] * lax.rsqrt(var[i] + WKV_GN_EPS) * lnw_ref[p] + lnb_ref[p] + bonus[i] for i, p in enumerate(ps)]
        for i, p in enumerate(ps):
            o_ref[0, :, lane_blk(p)] = o[i]

    @pl.when(c == pl.num_programs(1) - 1)
    def _():
        sfin_ref[0] = s_ref[...]


def _pairs(vec):
    return vec.reshape(N_PAIRS, 1, LANES)


def _wkv_scan(rw_s, w0, w2p, a0, a2p, k_k, k_a, s0, o_f, r_k, ln_w, ln_b, rev):
    b, l, w = rw_s.shape
    n = WKV_CHUNK
    nc = l // n
    cidx = (lambda c: nc - 1 - c) if rev else (lambda c: c)
    final = o_f is not None
    vec_spec = pl.BlockSpec((N_PAIRS, 1, LANES), lambda i, c: (0, 0, 0))
    st_spec = pl.BlockSpec((1, N_PAIRS, LANES, LANES), lambda i, c: (i, 0, 0, 0))
    in_specs = [pl.BlockSpec((1, n, w), lambda i, c: (i, cidx(c), 0)),
                pl.BlockSpec((1, WKV_WIDTH), lambda i, c: (0, 0)),
                pl.BlockSpec((LANES, WKV_WIDTH), lambda i, c: (0, 0)),
                pl.BlockSpec((1, WKV_WIDTH), lambda i, c: (0, 0)),
                pl.BlockSpec((LANES, WKV_WIDTH), lambda i, c: (0, 0)),
                vec_spec, vec_spec]
    args = [rw_s, w0.reshape(1, WKV_WIDTH), w2p, a0.reshape(1, WKV_WIDTH), a2p, _pairs(k_k), _pairs(k_a)]
    if s0 is not None:
        in_specs.append(st_spec)
        args.append(s0)
    if final:
        in_specs += [pl.BlockSpec((1, n, WKV_WIDTH), lambda i, c: (i, cidx(c), 0)), vec_spec, vec_spec, vec_spec]
        args += [o_f, _pairs(r_k), _pairs(ln_w), _pairs(ln_b)]
    kern = functools.partial(_wkv_kernel, rev=rev, has_init=s0 is not None, final=final)
    return pl.pallas_call(
        kern,
        grid=(b, nc),
        in_specs=in_specs,
        out_specs=[pl.BlockSpec((1, n, WKV_WIDTH), lambda i, c: (i, cidx(c), 0)), st_spec],
        out_shape=[jax.ShapeDtypeStruct((b, l, WKV_WIDTH), F32),
                   jax.ShapeDtypeStruct((b, N_PAIRS, LANES, LANES), F32)],
        scratch_shapes=[pltpu.VMEM((N_PAIRS, LANES, LANES), F32)],
        compiler_params=_cparams(("arbitrary", "arbitrary")),
        name="wkv_bwd" if rev else "wkv_fwd",
    )(*args)


def _final_kernel(y_ref, za_ref, o_ref, zb_ref, ga_ref, gb_ref, x_ref, mod_ref,
                  ng_ref, pa_ref, pb_ref, wo_ref, fg_ref, out_ref):
    y = y_ref[...] * _silu(za_ref[...])
    y = y * lax.rsqrt(jnp.mean(y * y, axis=-1, keepdims=True) + NORM_EPS) * ng_ref[...]
    u_a = _dot(y.astype(BF16), pa_ref[...])
    u_b = _dot((o_ref[...] * _silu(zb_ref[...])).astype(BF16), pb_ref[...])
    m = _sigmoid(ga_ref[...]) * u_a + _sigmoid(gb_ref[...]) * u_b
    out = _dot(m.astype(BF16), wo_ref[...])
    gate = mod_ref[0][:, 2 * D_MODEL:]
    xo = x_ref[...] + gate * out
    out_ref[...] = xo * lax.rsqrt(jnp.mean(xo * xo, axis=-1, keepdims=True) + NORM_EPS) * fg_ref[...]


def _final(y2, p_main, o2, x2, mod3, ssm_norm_g, p_a, p_b, w_out, final_g, row0, tiles_per_row, tm):
    t = x2.shape[0]
    const = lambda i: (0, 0)
    w1 = D_MODEL
    return pl.pallas_call(
        _final_kernel,
        grid=(t // tm,),
        in_specs=[pl.BlockSpec((tm, SSM_D_INNER), lambda i: (i, 0)),
                  pl.BlockSpec((tm, SSM_D_INNER), lambda i: (i, 0)),
                  pl.BlockSpec((tm, w1), lambda i: (i, 0)),
                  pl.BlockSpec((tm, w1), lambda i: (i, (2 * SSM_D_INNER + XBC_W) // w1)),
                  pl.BlockSpec((tm, w1), lambda i: (i, SSM_D_INNER // w1)),
                  pl.BlockSpec((tm, w1), lambda i: (i, SSM_D_INNER // w1 + 1)),
                  pl.BlockSpec((tm, w1), lambda i: (i, 0)),
                  pl.BlockSpec((1, 1, 3 * D_MODEL), lambda i: (row0 + i // tiles_per_row, 0, 0)),
                  pl.BlockSpec((1, SSM_D_INNER), const),
                  pl.BlockSpec((SSM_D_INNER, D_MODEL), const),
                  pl.BlockSpec((WKV_WIDTH, D_MODEL), const),
                  pl.BlockSpec((D_MODEL, D_MODEL), const),
                  pl.BlockSpec((1, D_MODEL), const)],
        out_specs=pl.BlockSpec((tm, D_MODEL), lambda i: (i, 0)),
        out_shape=jax.ShapeDtypeStruct((t, D_MODEL), F32),
        compiler_params=_cparams(("arbitrary",)),
        name="merge_out",
    )(y2, p_main, o2, p_main, p_main, p_main, x2, mod3,
      ssm_norm_g.reshape(1, SSM_D_INNER), p_a, p_b, w_out, final_g.reshape(1, D_MODEL))


def _to_col_major(x):
    b, l, c = x.shape
    rows = l // GRID_W
    return x.reshape(b, rows, GRID_W, c).transpose(0, 2, 1, 3).reshape(b, l, c)


def _from_col_major(x):
    b, l, c = x.shape
    rows = l // GRID_W
    return x.reshape(b, GRID_W, rows, c).transpose(0, 2, 1, 3).reshape(b, l, c)


def _wkv_state_to_pairs(s):
    b = s.shape[0]
    s = s.reshape(b, N_PAIRS, 2, WKV_HEADSIZE, WKV_HEADSIZE)
    z = jnp.zeros_like(s[:, :, 0])
    top = jnp.concatenate([s[:, :, 0], z], axis=-1)
    bot = jnp.concatenate([z, s[:, :, 1]], axis=-1)
    return jnp.concatenate([top, bot], axis=-2)


def _wkv_state_from_pairs(s):
    b = s.shape[0]
    h = WKV_HEADSIZE
    return jnp.stack([s[:, :, :h, :h], s[:, :, h:, h:]], axis=2).reshape(b, WKV_HEADS, h, h)


def _pad_lanes(v):
    return jnp.pad(v, (0, LANES - v.shape[0])).reshape(1, LANES)


def _group(x, mod3, row0, grid, states, wts):
    b, l, _ = x.shape
    t = b * l
    tm = 256
    x2 = x.reshape(t, D_MODEL)
    rows_per_cond = l // tm if grid else t // tm
    h = _adaln(x2, mod3, wts["norm_g"], row0, rows_per_cond, tm)
    p_main = _project(h, wts["w_main"], tn=1664)
    h_rw = _to_col_major(h.reshape(b, l, D_MODEL)).reshape(t, D_MODEL) if grid else h
    p_rw = _project(h_rw, wts["w_rw"], tn=640)
    p_main3 = p_main.reshape(b, l, MAIN_W)

    xbc = _conv_silu(p_main3, wts["conv_w"], wts["conv_b"])
    s_f = s_b = None
    if states is not None:
        s_f = states[0].reshape(b, SSM_PAIRS, LANES, SSM_STATE)
        s_b = states[1].reshape(b, SSM_PAIRS, LANES, SSM_STATE)
    y_f, fs_f = _ssd_scan(xbc, p_main3, wts["alog"][0], wts["dtb"][0], s_f, None, None, rev=False)
    y, fs_b = _ssd_scan(xbc, p_main3, wts["alog"][1], wts["dtb"][1], s_b, y_f, wts["dskip"], rev=True)

    rw_s = _token_shift(p_rw.reshape(b, l, RWKV_SCAN_W), wts["shift_mu"])
    w_f = w_b = None
    if states is not None:
        w_f = _wkv_state_to_pairs(states[2])
        w_b = _wkv_state_to_pairs(states[3])
    o_f, fw_f = _wkv_scan(rw_s, wts["w0"][0], wts["w2p"][0], wts["a0"], wts["a2p"], wts["k_k"], wts["k_a"],
                          w_f, None, None, None, None, rev=False)
    o, fw_b = _wkv_scan(rw_s, wts["w0"][1], wts["w2p"][1], wts["a0"], wts["a2p"], wts["k_k"], wts["k_a"],
                        w_b, o_f, wts["r_k"], wts["ln_w"], wts["ln_b"], rev=True)
    if grid:
        o = _from_col_major(o)

    out = _final(y.reshape(t, SSM_D_INNER), p_main, o.reshape(t, WKV_WIDTH), x2, mod3, wts["ssm_norm_g"],
                 wts["p_a"], wts["p_b"], wts["w_out"], wts["final_g"], row0, rows_per_cond, tm)
    finals = (fs_f.reshape(b, SSM_HEADS, SSM_HEADDIM, SSM_STATE), fs_b.reshape(b, SSM_HEADS, SSM_HEADDIM, SSM_STATE),
              _wkv_state_from_pairs(fw_f), _wkv_state_from_pairs(fw_b))
    return out.reshape(b, l, D_MODEL), finals


def kernel(x_prompt, x_sample, state_ssm_fwd, state_ssm_bwd, state_wkv_fwd, state_wkv_bwd, c, c_ctx, w_mod, b_mod, norm_g, w_in, conv_w, conv_b, a_log, dt_bias, d_skip, ssm_norm_g, p_a, shift_mu, w0, w2, a0, a2, k_k, k_a, r_k, ln_w, ln_b, p_b, w_out, final_g):
    depth = w_mod.shape[0]
    assert depth == 1, "single-layer stack only"
    l0 = 0
    w_in0 = w_in[l0]
    zpad = jnp.zeros((WKV_RANK, WKV_WIDTH), F32)
    w_main = jnp.concatenate([w_in0[:, :ZA_END], w_in0[:, ZB_END:], w_in0[:, ZA_END:XBC_END], w_in0[:, RW_END:ZB_END],
                              w_in0[:, XBC_END:DT_END], jnp.zeros((D_MODEL, LANES - SSM_HEADS), F32)], axis=1)
    wts = {
        "norm_g": norm_g[l0],
        "w_main": w_main.astype(BF16),
        "w_rw": w_in0[:, DT_END:RW_END].astype(BF16),
        "conv_w": conv_w[l0], "conv_b": conv_b[l0],
        "alog": [_pad_lanes(a_log[l0, d]) for d in range(2)],
        "dtb": [_pad_lanes(dt_bias[l0, d]) for d in range(2)],
        "dskip": jnp.repeat(d_skip[l0], SSM_HEADDIM).reshape(1, SSM_D_INNER),
        "ssm_norm_g": ssm_norm_g[l0],
        "p_a": p_a[l0].astype(BF16), "p_b": p_b[l0].astype(BF16), "w_out": w_out[l0].astype(BF16),
        "shift_mu": shift_mu[l0],
        "w0": w0[l0],
        "w2p": [jnp.concatenate([w2[l0, d], zpad], axis=0).astype(BF16) for d in range(2)],
        "a0": a0[l0],
        "a2p": jnp.concatenate([zpad, a2[l0]], axis=0).astype(BF16),
        "k_k": k_k[l0], "k_a": k_a[l0], "r_k": r_k[l0], "ln_w": ln_w[l0], "ln_b": ln_b[l0],
        "final_g": final_g,
    }
    nb = c.shape[0]
    cond8 = jnp.concatenate([c_ctx[None, :], c, jnp.zeros((8 - 1 - nb, D_MODEL), F32)], axis=0)
    mod3 = _modulation(cond8, w_mod[l0], b_mod[l0]).reshape(8, 1, 3 * D_MODEL)

    y_prompt, (sf, sb, wf, wb) = _group(x_prompt, mod3, 0, False, None, wts)
    lat_states = (state_ssm_fwd[:, l0], state_ssm_bwd[:, l0], state_wkv_fwd[:, l0], state_wkv_bwd[:, l0])
    y_sample, _ = _group(x_sample, mod3, 1, True, lat_states, wts)
    return (y_prompt, y_sample, sf[:, None], sb[:, None], wf[:, None], wb[:, None])
```

```python
import functools
import math

import jax
import jax.numpy as jnp
from jax import lax
from jax.experimental import pallas as pl
from jax.experimental.pallas import tpu as pltpu

F32 = jnp.float32
BF16 = jnp.bfloat16
HI = lax.Precision.HIGHEST

D_MODEL = 1024
GRID_W = 64
NORM_EPS = 1e-6
SSM_D_INNER = 2048
SSM_HEADDIM = 64
SSM_HEADS = 32
SSM_GROUPS = 4
SSM_STATE = 128
SSM_CHUNK = 128
WKV_WIDTH = 1024
WKV_HEADSIZE = 64
WKV_HEADS = 16
WKV_RANK = 64
WKV_GN_EPS = 64e-5
WKV_CHUNK = 64
XBC_W = SSM_D_INNER + 2 * SSM_GROUPS * SSM_STATE
RWKV_SCAN_W = 3 * WKV_WIDTH + 2 * WKV_RANK
ZA_END = SSM_D_INNER
XBC_END = ZA_END + XBC_W
DT_END = XBC_END + SSM_HEADS
RW_END = DT_END + RWKV_SCAN_W
ZB_END = RW_END + WKV_WIDTH
IN_W = ZB_END + 2 * D_MODEL

LANES = 128
N_PAIRS = WKV_HEADS // 2
SSM_PAIRS = SSM_HEADS // 2
WKV_SUB = 2
HALO = 16
GL_OFF = SSM_D_INNER
ZB_OFF = GL_OFF + 2 * D_MODEL
DT_OFF = ZB_OFF + WKV_WIDTH
PLAIN_TN = 14 * LANES
PLAIN_W = 3 * PLAIN_TN
VMEM_LIMIT = 48 * 1024 * 1024


def _cparams(sem):
    return pltpu.CompilerParams(dimension_semantics=sem, vmem_limit_bytes=VMEM_LIMIT)


def _sigmoid(x):
    return 1.0 / (1.0 + jnp.exp(-x))


def _silu(x):
    return x * _sigmoid(x)


def _softplus(x):
    return jnp.maximum(x, 0.0) + jnp.log(1.0 + jnp.exp(-jnp.abs(x)))


def _dot(a, b, precision=None):
    return jnp.dot(a, b, preferred_element_type=F32, precision=precision)


def _dot_nt(a, b, precision=None):
    return lax.dot_general(a, b, (((1,), (1,)), ((), ())), preferred_element_type=F32, precision=precision)


def _bdot(a, b):
    return _dot(a.astype(BF16), b.astype(BF16))


def _bdot_nt(a, b):
    return _dot_nt(a.astype(BF16), b.astype(BF16))


def _mod_kernel(c_ref, w_ref, b_ref, o_ref):
    c = c_ref[...]
    o_ref[...] = _dot(_silu(c), w_ref[...], HI) + b_ref[...]


def _modulation(cond8, w_mod, b_mod):
    n = w_mod.shape[1]
    tn = 1024
    return pl.pallas_call(
        _mod_kernel,
        grid=(n // tn,),
        in_specs=[pl.BlockSpec((8, D_MODEL), lambda j: (0, 0)),
                  pl.BlockSpec((D_MODEL, tn), lambda j: (0, j)),
                  pl.BlockSpec((1, tn), lambda j: (0, j))],
        out_specs=pl.BlockSpec((8, tn), lambda j: (0, j)),
        out_shape=jax.ShapeDtypeStruct((8, n), F32),
        compiler_params=_cparams(("arbitrary",)),
        name="modulation",
    )(cond8, w_mod, b_mod.reshape(1, n))


def _h_kernel(x_ref, mod_ref, g_ref, h_ref):
    x = x_ref[...]
    y = x * lax.rsqrt(jnp.mean(x * x, axis=-1, keepdims=True) + NORM_EPS) * g_ref[...]
    m = mod_ref[0]
    shift = m[:, :D_MODEL]
    scale = m[:, D_MODEL:2 * D_MODEL]
    h_ref[...] = (y * (1.0 + scale) + shift).astype(BF16)


def _adaln(x2, mod3, norm_g, row0, tiles_per_row, tm):
    t = x2.shape[0]
    return pl.pallas_call(
        _h_kernel,
        grid=(t // tm,),
        in_specs=[pl.BlockSpec((tm, D_MODEL), lambda i: (i, 0)),
                  pl.BlockSpec((1, 1, 3 * D_MODEL), lambda i: (row0 + i // tiles_per_row, 0, 0)),
                  pl.BlockSpec((1, D_MODEL), lambda i: (0, 0))],
        out_specs=pl.BlockSpec((tm, D_MODEL), lambda i: (i, 0)),
        out_shape=jax.ShapeDtypeStruct((t, D_MODEL), BF16),
        compiler_params=_cparams(("arbitrary",)),
        name="adaln_norm",
    )(x2, mod3, norm_g.reshape(1, D_MODEL))


def _proj_kernel(*refs, mode, tm, seq_len):
    if mode == "plain":
        h_ref, w_ref, o_ref = refs
        o_ref[...] = _dot(h_ref[...], w_ref[...])
        return
    halo = tm != seq_len
    cw, rb_rows = EPI_BLOCK[mode]
    n_cb = n_params = None
    if halo:
        h_ref, hp_ref, hn_ref, w_ref = refs[:4]
        n_cb = w_ref.shape[1] // cw
        params = refs[4:len(refs) - 1 - n_cb]
        lhs = jnp.concatenate([h_ref[...], hp_ref[...], hn_ref[...]], axis=0)
        start = pl.program_id(1) * tm
        keep_prev = jnp.where((start & (seq_len - 1)) != 0, 1.0, 0.0)
        keep_next = jnp.where(((start + tm) & (seq_len - 1)) != 0, 1.0, 0.0)
    else:
        h_ref, w_ref = refs[:2]
        n_cb = w_ref.shape[1] // cw
        params = refs[2:len(refs) - 1 - n_cb]
        lhs = h_ref[...]
    o_ref = refs[len(refs) - 1 - n_cb]
    p_scrs = refs[len(refs) - n_cb:]
    pad = 8
    sub = lax.broadcasted_iota(jnp.int32, (pad, cw), 0)
    nb = rb_rows // pad
    sub3 = lax.broadcasted_iota(jnp.int32, (nb, pad, cw), 1)

    def matmul_block(cb):
        cols = slice(cb * cw, (cb + 1) * cw)
        p_all = _dot(lhs, w_ref[:, cols])
        p_scr = p_scrs[cb]
        p_scr[pad:pad + tm, :] = p_all[:tm]
        if halo:
            prow = p_all[tm + HALO - 1:tm + HALO] * keep_prev
            nrow = p_all[tm + HALO:tm + HALO + 1] * keep_next
            p_scr[0:pad, :] = jnp.where(sub == pad - 1, prow, 0.0)
            p_scr[pad + tm:2 * pad + tm, :] = jnp.where(sub == 0, nrow, 0.0)
        else:
            p_scr[0:pad, :] = jnp.zeros((pad, cw), F32)
            p_scr[pad + tm:2 * pad + tm, :] = jnp.zeros((pad, cw), F32)

    def epilogue_block(cb):
        cols = slice(cb * cw, (cb + 1) * cw)
        for rb in range(tm // rb_rows):
            r0 = pad + rb * rb_rows
            rs = slice(rb * rb_rows, (rb + 1) * rb_rows)
            win = p_scrs[cb][r0 - pad:r0 + rb_rows + pad, :].reshape(nb + 2, pad, cw)
            cur = win[1:nb + 1]
            down = pltpu.roll(win[:nb + 1], 1, 1)
            up = pltpu.roll(win[1:], pad - 1, 1)
            prev = jnp.where(sub3 == 0, down[:nb], down[1:])
            nxt = jnp.where(sub3 == pad - 1, up[1:], up[:nb])
            if mode == "conv":
                cw_ref, cb_ref = params
                out = _silu(cw_ref[0:1, cols] * prev + cw_ref[1:2, cols] * cur
                            + cw_ref[2:3, cols] * nxt + cb_ref[:, cols])
            else:
                (mu_ref,) = params
                out = cur + mu_ref[:, cols] * (0.5 * (prev + nxt) - cur)
            o_ref[rs, cols] = out.reshape(rb_rows, cw)

    matmul_block(0)
    for cb in range(n_cb):
        epilogue_block(cb)
        if cb + 1 < n_cb:
            matmul_block(cb + 1)


EPI_BLOCK = {"conv": (256, 32), "shift": (640, 16)}


def _project(h, w, tn, mode="plain", seq_len=None, params=(), tm=512):
    t, k = h.shape
    n = w.shape[1]
    scratch = []
    if mode != "plain":
        tm = min(tm, seq_len)
        assert seq_len & (seq_len - 1) == 0 and seq_len % tm == 0 and tm % HALO == 0
        cw = EPI_BLOCK[mode][0]
        scratch = [pltpu.VMEM((tm + 16, cw), F32)] * (tn // cw)
    in_specs = [pl.BlockSpec((tm, k), lambda j, i: (i, 0))]
    args = [h]
    if mode != "plain" and tm != seq_len:
        per = tm // HALO
        in_specs += [pl.BlockSpec((HALO, k), lambda j, i: (jnp.maximum(i * per - 1, 0), 0)),
                     pl.BlockSpec((HALO, k), lambda j, i: (jnp.minimum((i + 1) * per, t // HALO - 1), 0))]
        args += [h, h]
    in_specs.append(pl.BlockSpec((k, tn), lambda j, i: (0, j)))
    args.append(w)
    for prm in params:
        in_specs.append(pl.BlockSpec((prm.shape[0], tn), lambda j, i: (0, j)))
        args.append(prm)
    return pl.pallas_call(
        functools.partial(_proj_kernel, mode=mode, tm=tm, seq_len=seq_len),
        grid=(n // tn, t // tm),
        in_specs=in_specs,
        out_specs=pl.BlockSpec((tm, tn), lambda j, i: (i, j)),
        out_shape=jax.ShapeDtypeStruct((t, n), F32),
        scratch_shapes=scratch,
        compiler_params=_cparams(("arbitrary", "arbitrary")),
        name="in_proj_" + mode,
    )(*args)


def _ssd_kernel(*refs, rev, has_init, has_prev):
    refs = list(refs)
    xs_ref, b_ref, c_ref, dt_ref, alog_ref, dtb_ref = refs[:6]
    pos = 6
    s0_ref = None
    if has_init:
        s0_ref = refs[pos]
        pos += 1
    yprev_ref = dskip_ref = None
    if has_prev:
        yprev_ref, dskip_ref = refs[pos], refs[pos + 1]
        pos += 2
    y_ref, sfin_ref, s_ref = refs[pos], refs[pos + 1], refs[pos + 2]

    q = SSM_CHUNK
    c = pl.program_id(1)

    @pl.when(c == 0)
    def _():
        if has_init:
            s_ref[...] = s0_ref[0]
        else:
            s_ref[...] = jnp.zeros_like(s_ref)

    ii = lax.broadcasted_iota(jnp.int32, (q, q), 0)
    jj = lax.broadcasted_iota(jnp.int32, (q, q), 1)
    incl = (jj >= ii) if rev else (jj <= ii)
    lane_lo = jj < SSM_HEADDIM
    row_lo = ii < SSM_HEADDIM
    last = 0 if rev else q - 1

    dt = _softplus(dt_ref[0] + dtb_ref[...])
    a = dt * (-jnp.exp(alog_ref[...]))
    cs = _dot(incl.astype(F32), a, HI)
    cs_t = cs.T

    for g in range(SSM_GROUPS):
        bm = b_ref[0, :, g * SSM_STATE:(g + 1) * SSM_STATE].astype(BF16)
        cm = c_ref[0, :, g * SSM_STATE:(g + 1) * SSM_STATE].astype(BF16)
        cb = _dot_nt(cm, bm)
        for jp in range(SSM_PAIRS // SSM_GROUPS):
            p = g * (SSM_PAIRS // SSM_GROUPS) + jp
            h0, h1 = 2 * p, 2 * p + 1
            lanes = slice(p * LANES, (p + 1) * LANES)
            xs = xs_ref[0, :, lanes]
            cs_pair = jnp.where(lane_lo, cs[:, h0:h0 + 1], cs[:, h1:h1 + 1])
            dt_pair = jnp.where(lane_lo, dt[:, h0:h0 + 1], dt[:, h1:h1 + 1])
            xdt = xs * dt_pair
            xdt_b = xdt.astype(BF16)
            y_heads = []
            for h in (h0, h1):
                seg = cs[:, h:h + 1] - cs_t[h:h + 1, :]
                lm = jnp.where(incl, jnp.exp(jnp.minimum(seg, 0.0)), 0.0)
                y_heads.append(_dot((cb * lm).astype(BF16), xdt_b))
            y_diag = jnp.where(lane_lo, y_heads[0], y_heads[1])
            s_pair = s_ref[p]
            y_off = _dot_nt(cm, s_pair.astype(BF16)) * jnp.exp(cs_pair)
            y = y_diag + y_off
            if has_prev:
                y = y + yprev_ref[0, :, lanes] + dskip_ref[:, lanes] * xs
            y_ref[0, :, lanes] = y
            cs_end = cs_pair[last:last + 1, :]
            decay = jnp.exp(cs_end - cs_pair)
            upd = _dot((xdt * decay).T.astype(BF16), bm)
            end_col = jnp.where(row_lo, cs_t[h0:h0 + 1, last:last + 1], cs_t[h1:h1 + 1, last:last + 1])
            s_ref[p] = s_pair * jnp.exp(end_col) + upd

    @pl.when(c == pl.num_programs(1) - 1)
    def _():
        sfin_ref[0] = s_ref[...]


def _ssd_scan(xbc, p_plain3, alog, dtb, s0, yprev, dskip, rev):
    b, l, _ = xbc.shape
    nc = l // SSM_CHUNK
    q = SSM_CHUNK
    cidx = (lambda c: nc - 1 - c) if rev else (lambda c: c)
    dt_blk = DT_OFF // LANES
    in_specs = [pl.BlockSpec((1, q, SSM_D_INNER), lambda i, c: (i, cidx(c), 0)),
                pl.BlockSpec((1, q, 512), lambda i, c: (i, cidx(c), SSM_D_INNER // 512)),
                pl.BlockSpec((1, q, 512), lambda i, c: (i, cidx(c), SSM_D_INNER // 512 + 1)),
                pl.BlockSpec((1, q, LANES), lambda i, c: (i, cidx(c), dt_blk)),
                pl.BlockSpec((1, LANES), lambda i, c: (0, 0)),
                pl.BlockSpec((1, LANES), lambda i, c: (0, 0))]
    args = [xbc, xbc, xbc, p_plain3, alog, dtb]
    if s0 is not None:
        in_specs.append(pl.BlockSpec((1, SSM_PAIRS, LANES, SSM_STATE), lambda i, c: (i, 0, 0, 0)))
        args.append(s0)
    if yprev is not None:
        in_specs.append(pl.BlockSpec((1, q, SSM_D_INNER), lambda i, c: (i, cidx(c), 0)))
        in_specs.append(pl.BlockSpec((1, SSM_D_INNER), lambda i, c: (0, 0)))
        args += [yprev, dskip]
    kern = functools.partial(_ssd_kernel, rev=rev, has_init=s0 is not None, has_prev=yprev is not None)
    return pl.pallas_call(
        kern,
        grid=(b, nc),
        in_specs=in_specs,
        out_specs=[pl.BlockSpec((1, q, SSM_D_INNER), lambda i, c: (i, cidx(c), 0)),
                   pl.BlockSpec((1, SSM_PAIRS, LANES, SSM_STATE), lambda i, c: (i, 0, 0, 0))],
        out_shape=[jax.ShapeDtypeStruct((b, l, SSM_D_INNER), F32),
                   jax.ShapeDtypeStruct((b, SSM_PAIRS, LANES, SSM_STATE), F32)],
        scratch_shapes=[pltpu.VMEM((SSM_PAIRS, LANES, SSM_STATE), F32)],
        compiler_params=_cparams(("arbitrary", "arbitrary")),
        name="ssd_bwd" if rev else "ssd_fwd",
    )(*args)


PREC = {
    "lora": (1, 1),
    "cum": (1, 2),
    "seg": (2, 1),
    "amat": (1, 1),
    "inv": (1, 1),
    "state": (1, 1),
    "gv": (1, 1),
    "u": (1, 1),
    "o": (1, 1),
    "upd": (1, 1),
}


def _pieces(x, n):
    out = []
    for i in range(n):
        h = x.astype(BF16)
        out.append(h)
        if i + 1 < n:
            x = x - h.astype(F32)
    return out


def _mm(a_pieces, b_pieces, nt=False):
    dot = _dot_nt if nt else _dot
    depth = max(len(a_pieces), len(b_pieces))
    acc = None
    for i in reversed(range(len(a_pieces))):
        for j in reversed(range(len(b_pieces))):
            if i + j < depth:
                t = dot(a_pieces[i], b_pieces[j])
                acc = t if acc is None else acc + t
    return acc


def _block_diag(y, m_lo, m_hi):
    return jnp.concatenate([y * m_lo, y * m_hi], axis=0)


def _wkv_kernel(*refs, rev, has_init, final):
    refs = list(refs)
    (rw_ref, w0_ref, w2_ref, a0_ref, a2_ref, kk_ref, ka_ref) = refs[:7]
    pos = 7
    s0_ref = None
    if has_init:
        s0_ref = refs[pos]
        pos += 1
    of_ref = rk_ref = lnw_ref = lnb_ref = None
    if final:
        of_ref, rk_ref, lnw_ref, lnb_ref = refs[pos:pos + 4]
        pos += 4
    o_ref, sfin_ref, s_ref = refs[pos], refs[pos + 1], refs[pos + 2]

    n = WKV_CHUNK
    c = pl.program_id(1)

    @pl.when(c == 0)
    def _():
        if has_init:
            s_ref[...] = s0_ref[0]
        else:
            s_ref[...] = jnp.zeros_like(s_ref)

    xl = rw_ref[0, :, 3 * WKV_WIDTH:3 * WKV_WIDTH + LANES]
    a_full = _sigmoid(a0_ref[...] + _mm(_pieces(xl, PREC["lora"][0]), [a2_ref[...]]))
    xw = w0_ref[...] + _mm(_pieces(jnp.tanh(xl), PREC["lora"][0]), [w2_ref[...]])
    lw_full = -math.exp(-0.5) * _sigmoid(xw)

    ti = lax.broadcasted_iota(jnp.int32, (n, LANES), 0)
    li = lax.broadcasted_iota(jnp.int32, (n, LANES), 1)
    si = li & (WKV_HEADSIZE - 1)
    strict = (si > ti) if rev else (si < ti)
    incl = (si >= ti) if rev else (si <= ti)
    eye = (si == ti).astype(F32)
    m_lo = (li < WKV_HEADSIZE).astype(BF16)
    m_hi = (li >= WKV_HEADSIZE).astype(BF16)
    tq = lax.broadcasted_iota(jnp.int32, (n, n), 0)
    sq = lax.broadcasted_iota(jnp.int32, (n, n), 1)
    tri = [((sq >= tq) if rev else (sq <= tq)).astype(BF16)]
    r2 = lax.broadcasted_iota(jnp.int32, (LANES, LANES), 0)
    c2 = lax.broadcasted_iota(jnp.int32, (LANES, LANES), 1)
    same_head = (r2 < WKV_HEADSIZE) == (c2 < WKV_HEADSIZE)
    ones_bd = [same_head.astype(BF16)]
    last = 0 if rev else n - 1

    def level_mask(m):
        same = (ti >> (m.bit_length())) == (si >> (m.bit_length()))
        t_hi = (ti & m) != 0
        s_hi = (si & m) != 0
        if rev:
            return same & jnp.logical_not(t_hi) & s_hi
        return same & t_hi & jnp.logical_not(s_hi)

    levels = []
    m = 2
    while m < n:
        levels.append(level_mask(m).astype(BF16))
        m *= 2
    level1 = level_mask(1)

    def bd(pieces):
        return [_block_diag(q, m_lo, m_hi) for q in pieces]

    def seg_sum(x):
        return _mm(_pieces(x, PREC["seg"][0]), ones_bd)

    def lane_blk(p, base=0):
        return slice(base + p * LANES, base + (p + 1) * LANES)

    def rows(j):
        return slice(j * n, (j + 1) * n)

    subs = list(range(WKV_SUB))[::-1] if rev else list(range(WKV_SUB))
    chains = [(j, p) for j in subs for p in range(N_PAIRS)]
    idx = range(len(chains))
    r = [rw_ref[0, rows(j), lane_blk(p)] for j, p in chains]
    k = [rw_ref[0, rows(j), lane_blk(p, WKV_WIDTH)] for j, p in chains]
    v = [rw_ref[0, rows(j), lane_blk(p, 2 * WKV_WIDTH)] for j, p in chains]
    a = [a_full[rows(j), lane_blk(p)] for j, p in chains]
    lw = [lw_full[rows(j), lane_blk(p)] for j, p in chains]
    kkr = [k[i] * kk_ref[p] for i, (j, p) in enumerate(chains)]
    ss = [seg_sum(q * q) for q in kkr]
    kk = [kkr[i] / jnp.maximum(jnp.sqrt(ss[i]), 1e-12) for i in idx]
    kmod = [k[i] * (1.0 + (a[i] - 1.0) * ka_ref[p]) for i, (j, p) in enumerate(chains)]
    kka = [kk[i] * a[i] for i in idx]

    lg = [_mm(tri, _pieces(q, PREC["cum"][1])) for q in lw]
    g_inv = [jnp.exp(-q) for q in lg]
    rh = [r[i] * jnp.exp(lg[i]) for i in idx]
    kh = [kmod[i] * g_inv[i] for i in idx]
    ah = [kka[i] * g_inv[i] for i in idx]
    bh = [-kk[i] * jnp.exp(lg[i] - lw[i]) for i in idx]
    g_end = [jnp.exp(q[last:last + 1, :]) for q in lg]

    pa, pb = PREC["amat"]
    lhs = [_pieces(jnp.concatenate([bh[i], rh[i]], axis=0), max(pa, PREC["state"][0])) for i in idx]
    aa = [_mm(lhs[i][:pa], bd(_pieces(ah[i], pb)), nt=True) for i in idx]
    ak = [_mm(lhs[i][:pa], bd(_pieces(kh[i], pb)), nt=True) for i in idx]
    a_ab = [jnp.where(strict, q[:n], 0.0) for q in aa]
    a_ra = [jnp.where(incl, q[n:], 0.0) for q in aa]
    a_bk = [jnp.where(strict, q[:n], 0.0) for q in ak]
    a_rk = [jnp.where(incl, q[n:], 0.0) for q in ak]

    pa, pb = PREC["inv"]
    ab_pieces = [_pieces(q, pb) for q in a_ab]
    x = [eye + jnp.where(level1, q, 0.0) for q in a_ab]
    for lm in levels:
        e = [bd([q * lm for q in ab_pieces[i]]) for i in idx]
        pm = [_mm(_pieces(x[i], pa), e[i]) for i in idx]
        x = [x[i] + _mm(_pieces(pm[i], pa), bd(_pieces(x[i], pb))) for i in idx]
    pa, pb = PREC["gv"]
    gv = [_mm(_pieces(jnp.concatenate([a_bk[i], a_rk[i]], axis=0), pa), bd(_pieces(v[i], pb))) for i in idx]
    ake = [jnp.concatenate([ah[i], kh[i]], axis=0) * g_end[i] for i in idx]

    o = [None] * len(chains)
    for jj in range(WKV_SUB):
        ids = list(range(jj * N_PAIRS, (jj + 1) * N_PAIRS))
        s_bd = {i: s_ref[chains[i][1]] for i in ids}
        x0r = {i: _mm(lhs[i][:PREC["state"][0]], _pieces(s_bd[i], PREC["state"][1]), nt=True) for i in ids}
        pa, pb = PREC["u"]
        u = {i: _mm(_pieces(x[i], pa), bd(_pieces(x0r[i][:n] + gv[i][:n], pb))) for i in ids}
        pa, pb = PREC["o"]
        for i in ids:
            o[i] = x0r[i][n:] + gv[i][n:] + _mm(_pieces(a_ra[i], pa), bd(_pieces(u[i], pb)))
        pa, pb = PREC["upd"]
        uvt = {i: jnp.concatenate([u[i], v[i]], axis=0).T for i in ids}
        upd = {i: _mm(_pieces(uvt[i], pa), _pieces(ake[i], pb)) for i in ids}
        for i in ids:
            s_ref[chains[i][1]] = s_bd[i] * g_end[i] + jnp.where(same_head, upd[i], 0.0)

    if final:
        o = [o[i] + of_ref[0, rows(j), lane_blk(p)] for i, (j, p) in enumerate(chains)]
        mu = [seg_sum(q) * (1.0 / WKV_HEADSIZE) for q in o]
        d = [o[i] - mu[i] for i in idx]
        var = [seg_sum(q * q) * (1.0 / WKV_HEADSIZE) for q in d]
        bonus = [seg_sum(r[i] * kmod[i] * rk_ref[p]) * v[i] for i, (j, p) in enumerate(chains)]
        o = [d[i] * lax.rsqrt(var[i] + WKV_GN_EPS) * lnw_ref[p] + lnb_ref[p] + bonus[i]
             for i, (j, p) in enumerate(chains)]
    for i, (j, p) in enumerate(chains):
        o_ref[0, rows(j), lane_blk(p)] = o[i]

    @pl.when(c == pl.num_programs(1) - 1)
    def _():
        sfin_ref[0] = s_ref[...]


def _pairs(vec):
    return vec.reshape(N_PAIRS, 1, LANES)


def _wkv_scan(rw_s, w0, w2p, a0, a2p, k_k, k_a, s0, o_f, r_k, ln_w, ln_b, rev):
    b, l, w = rw_s.shape
    n = WKV_CHUNK * WKV_SUB
    nc = l // n
    cidx = (lambda c: nc - 1 - c) if rev else (lambda c: c)
    final = o_f is not None
    vec_spec = pl.BlockSpec((N_PAIRS, 1, LANES), lambda i, c: (0, 0, 0))
    st_spec = pl.BlockSpec((1, N_PAIRS, LANES, LANES), lambda i, c: (i, 0, 0, 0))
    in_specs = [pl.BlockSpec((1, n, w), lambda i, c: (i, cidx(c), 0)),
                pl.BlockSpec((1, WKV_WIDTH), lambda i, c: (0, 0)),
                pl.BlockSpec((LANES, WKV_WIDTH), lambda i, c: (0, 0)),
                pl.BlockSpec((1, WKV_WIDTH), lambda i, c: (0, 0)),
                pl.BlockSpec((LANES, WKV_WIDTH), lambda i, c: (0, 0)),
                vec_spec, vec_spec]
    args = [rw_s, w0.reshape(1, WKV_WIDTH), w2p, a0.reshape(1, WKV_WIDTH), a2p, _pairs(k_k), _pairs(k_a)]
    if s0 is not None:
        in_specs.append(st_spec)
        args.append(s0)
    if final:
        in_specs += [pl.BlockSpec((1, n, WKV_WIDTH), lambda i, c: (i, cidx(c), 0)), vec_spec, vec_spec, vec_spec]
        args += [o_f, _pairs(r_k), _pairs(ln_w), _pairs(ln_b)]
    kern = functools.partial(_wkv_kernel, rev=rev, has_init=s0 is not None, final=final)
    return pl.pallas_call(
        kern,
        grid=(b, nc),
        in_specs=in_specs,
        out_specs=[pl.BlockSpec((1, n, WKV_WIDTH), lambda i, c: (i, cidx(c), 0)), st_spec],
        out_shape=[jax.ShapeDtypeStruct((b, l, WKV_WIDTH), F32),
                   jax.ShapeDtypeStruct((b, N_PAIRS, LANES, LANES), F32)],
        scratch_shapes=[pltpu.VMEM((N_PAIRS, LANES, LANES), F32)],
        compiler_params=_cparams(("arbitrary", "arbitrary")),
        name="wkv_bwd" if rev else "wkv_fwd",
    )(*args)


def _final_kernel(y_ref, za_ref, o_ref, zb_ref, ga_ref, gb_ref, x_ref, mod_ref,
                  ng_ref, pa_ref, pb_ref, wo_ref, fg_ref, out_ref):
    y = y_ref[...] * _silu(za_ref[...])
    y = y * lax.rsqrt(jnp.mean(y * y, axis=-1, keepdims=True) + NORM_EPS) * ng_ref[...]
    u_a = _dot(y.astype(BF16), pa_ref[...])
    u_b = _dot((o_ref[...] * _silu(zb_ref[...])).astype(BF16), pb_ref[...])
    m = _sigmoid(ga_ref[...]) * u_a + _sigmoid(gb_ref[...]) * u_b
    out = _dot(m.astype(BF16), wo_ref[...])
    gate = mod_ref[0][:, 2 * D_MODEL:]
    xo = x_ref[...] + gate * out
    out_ref[...] = xo * lax.rsqrt(jnp.mean(xo * xo, axis=-1, keepdims=True) + NORM_EPS) * fg_ref[...]


def _final(y2, p_plain, o2, x2, mod3, ssm_norm_g, p_a, p_b, w_out, final_g, row0, tiles_per_row, tm):
    t = x2.shape[0]
    const = lambda i: (0, 0)
    w1 = D_MODEL
    return pl.pallas_call(
        _final_kernel,
        grid=(t // tm,),
        in_specs=[pl.BlockSpec((tm, SSM_D_INNER), lambda i: (i, 0)),
                  pl.BlockSpec((tm, SSM_D_INNER), lambda i: (i, 0)),
                  pl.BlockSpec((tm, w1), lambda i: (i, 0)),
                  pl.BlockSpec((tm, w1), lambda i: (i, ZB_OFF // w1)),
                  pl.BlockSpec((tm, w1), lambda i: (i, GL_OFF // w1)),
                  pl.BlockSpec((tm, w1), lambda i: (i, GL_OFF // w1 + 1)),
                  pl.BlockSpec((tm, w1), lambda i: (i, 0)),
                  pl.BlockSpec((1, 1, 3 * D_MODEL), lambda i: (row0 + i // tiles_per_row, 0, 0)),
                  pl.BlockSpec((1, SSM_D_INNER), const),
                  pl.BlockSpec((SSM_D_INNER, D_MODEL), const),
                  pl.BlockSpec((WKV_WIDTH, D_MODEL), const),
                  pl.BlockSpec((D_MODEL, D_MODEL), const),
                  pl.BlockSpec((1, D_MODEL), const)],
        out_specs=pl.BlockSpec((tm, D_MODEL), lambda i: (i, 0)),
        out_shape=jax.ShapeDtypeStruct((t, D_MODEL), F32),
        compiler_params=_cparams(("arbitrary",)),
        name="merge_out",
    )(y2, p_plain, o2, p_plain, p_plain, p_plain, x2, mod3,
      ssm_norm_g.reshape(1, SSM_D_INNER), p_a, p_b, w_out, final_g.reshape(1, D_MODEL))


def _to_col_major(x):
    b, l, c = x.shape
    rows = l // GRID_W
    return x.reshape(b, rows, GRID_W, c).transpose(0, 2, 1, 3).reshape(b, l, c)


def _from_col_major(x):
    b, l, c = x.shape
    rows = l // GRID_W
    return x.reshape(b, GRID_W, rows, c).transpose(0, 2, 1, 3).reshape(b, l, c)


def _wkv_state_to_pairs(s):
    b = s.shape[0]
    s = s.reshape(b, N_PAIRS, 2, WKV_HEADSIZE, WKV_HEADSIZE)
    z = jnp.zeros_like(s[:, :, 0])
    top = jnp.concatenate([s[:, :, 0], z], axis=-1)
    bot = jnp.concatenate([z, s[:, :, 1]], axis=-1)
    return jnp.concatenate([top, bot], axis=-2)


def _wkv_state_from_pairs(s):
    b = s.shape[0]
    h = WKV_HEADSIZE
    return jnp.stack([s[:, :, :h, :h], s[:, :, h:, h:]], axis=2).reshape(b, WKV_HEADS, h, h)


def _pad_lanes(v):
    return jnp.pad(v, (0, LANES - v.shape[0])).reshape(1, LANES)


def _group(x, mod3, row0, grid, states, wts):
    b, l, _ = x.shape
    t = b * l
    tm = 256
    x2 = x.reshape(t, D_MODEL)
    rows_per_cond = l // tm if grid else t // tm
    h = _adaln(x2, mod3, wts["norm_g"], row0, rows_per_cond, tm)
    p_plain = _project(h, wts["w_plain"], tn=PLAIN_TN)
    p_plain3 = p_plain.reshape(b, l, PLAIN_W)
    h_rw = _to_col_major(h.reshape(b, l, D_MODEL)).reshape(t, D_MODEL) if grid else h

    xbc = _project(h, wts["w_xbc"], tn=XBC_W // 2, mode="conv", seq_len=l,
                   params=(wts["conv_w"], wts["conv_b"].reshape(1, XBC_W))).reshape(b, l, XBC_W)
    s_f = s_b = None
    if states is not None:
        s_f = states[0].reshape(b, SSM_PAIRS, LANES, SSM_STATE)
        s_b = states[1].reshape(b, SSM_PAIRS, LANES, SSM_STATE)
    y_f, fs_f = _ssd_scan(xbc, p_plain3, wts["alog"][0], wts["dtb"][0], s_f, None, None, rev=False)
    y, fs_b = _ssd_scan(xbc, p_plain3, wts["alog"][1], wts["dtb"][1], s_b, y_f, wts["dskip"], rev=True)

    rw_s = _project(h_rw, wts["w_rw"], tn=RWKV_SCAN_W // 5, mode="shift", seq_len=l,
                    params=(wts["shift_mu"].reshape(1, RWKV_SCAN_W),)).reshape(b, l, RWKV_SCAN_W)
    w_f = w_b = None
    if states is not None:
        w_f = _wkv_state_to_pairs(states[2])
        w_b = _wkv_state_to_pairs(states[3])
    o_f, fw_f = _wkv_scan(rw_s, wts["w0"][0], wts["w2p"][0], wts["a0"], wts["a2p"], wts["k_k"], wts["k_a"],
                          w_f, None, None, None, None, rev=False)
    o, fw_b = _wkv_scan(rw_s, wts["w0"][1], wts["w2p"][1], wts["a0"], wts["a2p"], wts["k_k"], wts["k_a"],
                        w_b, o_f, wts["r_k"], wts["ln_w"], wts["ln_b"], rev=True)
    if grid:
        o = _from_col_major(o)

    out = _final(y.reshape(t, SSM_D_INNER), p_plain, o.reshape(t, WKV_WIDTH), x2, mod3, wts["ssm_norm_g"],
                 wts["p_a"], wts["p_b"], wts["w_out"], wts["final_g"], row0, rows_per_cond, tm)
    finals = (fs_f.reshape(b, SSM_HEADS, SSM_HEADDIM, SSM_STATE), fs_b.reshape(b, SSM_HEADS, SSM_HEADDIM, SSM_STATE),
              _wkv_state_from_pairs(fw_f), _wkv_state_from_pairs(fw_b))
    return out.reshape(b, l, D_MODEL), finals


def kernel(x_prompt, x_sample, state_ssm_fwd, state_ssm_bwd, state_wkv_fwd, state_wkv_bwd, c, c_ctx, w_mod, b_mod, norm_g, w_in, conv_w, conv_b, a_log, dt_bias, d_skip, ssm_norm_g, p_a, shift_mu, w0, w2, a0, a2, k_k, k_a, r_k, ln_w, ln_b, p_b, w_out, final_g):
    depth = w_mod.shape[0]
    assert depth == 1, "single-layer stack only"
    l0 = 0
    w_in0 = w_in[l0]
    zpad = jnp.zeros((WKV_RANK, WKV_WIDTH), F32)
    w_plain = jnp.concatenate([w_in0[:, :ZA_END], w_in0[:, ZB_END:], w_in0[:, RW_END:ZB_END], w_in0[:, XBC_END:DT_END],
                               jnp.zeros((D_MODEL, PLAIN_W - DT_OFF - SSM_HEADS), F32)], axis=1)
    wts = {
        "norm_g": norm_g[l0],
        "w_plain": w_plain.astype(BF16),
        "w_xbc": w_in0[:, ZA_END:XBC_END].astype(BF16),
        "w_rw": w_in0[:, DT_END:RW_END].astype(BF16),
        "conv_w": conv_w[l0], "conv_b": conv_b[l0],
        "alog": [_pad_lanes(a_log[l0, d]) for d in range(2)],
        "dtb": [_pad_lanes(dt_bias[l0, d]) for d in range(2)],
        "dskip": jnp.repeat(d_skip[l0], SSM_HEADDIM).reshape(1, SSM_D_INNER),
        "ssm_norm_g": ssm_norm_g[l0],
        "p_a": p_a[l0].astype(BF16), "p_b": p_b[l0].astype(BF16), "w_out": w_out[l0].astype(BF16),
        "shift_mu": shift_mu[l0],
        "w0": w0[l0],
        "w2p": [jnp.concatenate([w2[l0, d], zpad], axis=0).astype(BF16) for d in range(2)],
        "a0": a0[l0],
        "a2p": jnp.concatenate([zpad, a2[l0]], axis=0).astype(BF16),
        "k_k": k_k[l0], "k_a": k_a[l0], "r_k": r_k[l0], "ln_w": ln_w[l0], "ln_b": ln_b[l0],
        "final_g": final_g,
    }
    nb = c.shape[0]
    cond8 = jnp.concatenate([c_ctx[None, :], c, jnp.zeros((8 - 1 - nb, D_MODEL), F32)], axis=0)
    mod3 = _modulation(cond8, w_mod[l0], b_mod[l0]).reshape(8, 1, 3 * D_MODEL)

    y_prompt, (sf, sb, wf, wb) = _group(x_prompt, mod3, 0, False, None, wts)
    lat_states = (state_ssm_fwd[:, l0], state_ssm_bwd[:, l0], state_wkv_fwd[:, l0], state_wkv_bwd[:, l0])
    y_sample, _ = _group(x_sample, mod3, 1, True, lat_states, wts)
    return (y_prompt, y_sample, sf[:, None], sb[:, None], wf[:, None], wb[:, None])
```

```python
import functools
import math

import jax
import jax.numpy as jnp
from jax import lax
from jax.experimental import pallas as pl
from jax.experimental.pallas import tpu as pltpu

F32 = jnp.float32
BF16 = jnp.bfloat16
HI = lax.Precision.HIGHEST

D_MODEL = 1024
GRID_W = 64
NORM_EPS = 1e-6
SSM_D_INNER = 2048
SSM_HEADDIM = 64
SSM_HEADS = 32
SSM_GROUPS = 4
SSM_STATE = 128
SSM_CHUNK = 128
WKV_WIDTH = 1024
WKV_HEADSIZE = 64
WKV_HEADS = 16
WKV_RANK = 64
WKV_GN_EPS = 64e-5
WKV_CHUNK = 64
XBC_W = SSM_D_INNER + 2 * SSM_GROUPS * SSM_STATE
RWKV_SCAN_W = 3 * WKV_WIDTH + 2 * WKV_RANK
ZA_END = SSM_D_INNER
XBC_END = ZA_END + XBC_W
DT_END = XBC_END + SSM_HEADS
RW_END = DT_END + RWKV_SCAN_W
ZB_END = RW_END + WKV_WIDTH
IN_W = ZB_END + 2 * D_MODEL

LANES = 128
N_PAIRS = WKV_HEADS // 2
SSM_PAIRS = SSM_HEADS // 2
WKV_SUB = 2
FINAL_TM = 512
ADALN_TM = 1024
HALO = 16
GL_OFF = SSM_D_INNER
ZB_OFF = GL_OFF + 2 * D_MODEL
DT_OFF = ZB_OFF + WKV_WIDTH
PLAIN_TN = 14 * LANES
PLAIN_W = 3 * PLAIN_TN
VMEM_LIMIT = 48 * 1024 * 1024


def _cparams(sem):
    return pltpu.CompilerParams(dimension_semantics=sem, vmem_limit_bytes=VMEM_LIMIT)


def _sigmoid(x):
    return 1.0 / (1.0 + jnp.exp2(x * (-1.0 / math.log(2.0))))


def _silu(x):
    return x * _sigmoid(x)


def _softplus(x):
    return jnp.maximum(x, 0.0) + jnp.log(1.0 + jnp.exp(-jnp.abs(x)))


def _dot(a, b, precision=None):
    return jnp.dot(a, b, preferred_element_type=F32, precision=precision)


def _dot_nt(a, b, precision=None):
    return lax.dot_general(a, b, (((1,), (1,)), ((), ())), preferred_element_type=F32, precision=precision)


def _bdot(a, b):
    return _dot(a.astype(BF16), b.astype(BF16))


def _bdot_nt(a, b):
    return _dot_nt(a.astype(BF16), b.astype(BF16))


def _mod_kernel(c_ref, w_ref, b_ref, o_ref):
    c = c_ref[...]
    o_ref[...] = _dot(_silu(c), w_ref[...], HI) + b_ref[...]


def _modulation(cond8, w_mod, b_mod):
    n = w_mod.shape[1]
    tn = 1024
    return pl.pallas_call(
        _mod_kernel,
        grid=(n // tn,),
        in_specs=[pl.BlockSpec((8, D_MODEL), lambda j: (0, 0)),
                  pl.BlockSpec((D_MODEL, tn), lambda j: (0, j)),
                  pl.BlockSpec((1, tn), lambda j: (0, j))],
        out_specs=pl.BlockSpec((8, tn), lambda j: (0, j)),
        out_shape=jax.ShapeDtypeStruct((8, n), F32),
        compiler_params=_cparams(("arbitrary",)),
        name="modulation",
    )(cond8, w_mod, b_mod.reshape(1, n))


def _h_kernel(x_ref, mod_ref, g_ref, h_ref):
    x = x_ref[...]
    y = x * lax.rsqrt(jnp.mean(x * x, axis=-1, keepdims=True) + NORM_EPS) * g_ref[...]
    m = mod_ref[0]
    shift = m[:, :D_MODEL]
    scale = m[:, D_MODEL:2 * D_MODEL]
    h_ref[...] = (y * (1.0 + scale) + shift).astype(BF16)


def _adaln(x2, mod3, norm_g, row0, tiles_per_row, tm):
    t = x2.shape[0]
    return pl.pallas_call(
        _h_kernel,
        grid=(t // tm,),
        in_specs=[pl.BlockSpec((tm, D_MODEL), lambda i: (i, 0)),
                  pl.BlockSpec((1, 1, 3 * D_MODEL), lambda i: (row0 + i // tiles_per_row, 0, 0)),
                  pl.BlockSpec((1, D_MODEL), lambda i: (0, 0))],
        out_specs=pl.BlockSpec((tm, D_MODEL), lambda i: (i, 0)),
        out_shape=jax.ShapeDtypeStruct((t, D_MODEL), BF16),
        compiler_params=_cparams(("arbitrary",)),
        name="adaln_norm",
    )(x2, mod3, norm_g.reshape(1, D_MODEL))


def _proj_kernel(*refs, mode, tm, seq_len):
    if mode == "plain":
        h_ref, w_ref, o_ref = refs
        o_ref[...] = _dot(h_ref[...], w_ref[...])
        return
    halo = tm != seq_len
    cw, rb_rows, row_chunk = EPI_BLOCK[mode]
    rc_rows = min(row_chunk, tm)
    if halo:
        h_ref, hp_ref, hn_ref, w_ref = refs[:4]
        n_cb = w_ref.shape[1] // cw
        params = refs[4:len(refs) - 1 - n_cb]
        start = pl.program_id(1) * tm
        keep_prev = jnp.where((start & (seq_len - 1)) != 0, 1.0, 0.0)
        keep_next = jnp.where(((start + tm) & (seq_len - 1)) != 0, 1.0, 0.0)
    else:
        h_ref, w_ref = refs[:2]
        n_cb = w_ref.shape[1] // cw
        params = refs[2:len(refs) - 1 - n_cb]
    o_ref = refs[len(refs) - 1 - n_cb]
    p_scrs = refs[len(refs) - n_cb:]
    pad = 8
    sub = lax.broadcasted_iota(jnp.int32, (pad, cw), 0)
    nb = rb_rows // pad
    sub3 = lax.broadcasted_iota(jnp.int32, (nb, pad, cw), 1)
    n_rc = tm // rc_rows

    def matmul_unit(cb, rc):
        cols = slice(cb * cw, (cb + 1) * cw)
        p_scr = p_scrs[cb]
        rows = slice(rc * rc_rows, (rc + 1) * rc_rows)
        if rc == 0 and halo:
            lhs = jnp.concatenate([h_ref[rows, :], hp_ref[...], hn_ref[...]], axis=0)
        else:
            lhs = h_ref[rows, :]
        p = _dot(lhs, w_ref[:, cols])
        p_scr[pad + rc * rc_rows:pad + (rc + 1) * rc_rows, :] = p[:rc_rows]
        if rc == 0 and halo:
            prow = p[rc_rows + HALO - 1:rc_rows + HALO] * keep_prev
            nrow = p[rc_rows + HALO:rc_rows + HALO + 1] * keep_next
            p_scr[0:pad, :] = jnp.where(sub == pad - 1, prow, 0.0)
            p_scr[pad + tm:2 * pad + tm, :] = jnp.where(sub == 0, nrow, 0.0)
        elif rc == 0:
            p_scr[0:pad, :] = jnp.zeros((pad, cw), F32)
            p_scr[pad + tm:2 * pad + tm, :] = jnp.zeros((pad, cw), F32)

    def epilogue_unit(cb, rc):
        cols = slice(cb * cw, (cb + 1) * cw)
        per = rc_rows // rb_rows
        for rb in range(rc * per, (rc + 1) * per):
            r0 = pad + rb * rb_rows
            rs = slice(rb * rb_rows, (rb + 1) * rb_rows)
            win = p_scrs[cb][r0 - pad:r0 + rb_rows + pad, :].reshape(nb + 2, pad, cw)
            cur = win[1:nb + 1]
            down = pltpu.roll(win[:nb + 1], 1, 1)
            up = pltpu.roll(win[1:], pad - 1, 1)
            prev = jnp.where(sub3 == 0, down[:nb], down[1:])
            nxt = jnp.where(sub3 == pad - 1, up[1:], up[:nb])
            if mode == "conv":
                cw_ref, cb_ref = params
                out = _silu(cw_ref[0:1, cols] * prev + cw_ref[1:2, cols] * cur
                            + cw_ref[2:3, cols] * nxt + cb_ref[:, cols])
            else:
                (mu_ref,) = params
                out = cur + mu_ref[:, cols] * (0.5 * (prev + nxt) - cur)
            o_ref[rs, cols] = out.reshape(rb_rows, cw)

    units = [(cb, rc) for cb in range(n_cb) for rc in range(n_rc)]
    matmul_unit(*units[0])
    for i, unit in enumerate(units):
        if i + 1 < len(units):
            matmul_unit(*units[i + 1])
        epilogue_unit(*unit)


EPI_BLOCK = {"conv": (256, 32, 128), "shift": (640, 16, 512)}


def _project(h, w, tn, mode="plain", seq_len=None, params=(), tm=512):
    t, k = h.shape
    n = w.shape[1]
    scratch = []
    if mode != "plain":
        tm = min(tm, seq_len)
        assert seq_len & (seq_len - 1) == 0 and seq_len % tm == 0 and tm % HALO == 0
        cw = EPI_BLOCK[mode][0]
        scratch = [pltpu.VMEM((tm + 16, cw), F32)] * (tn // cw)
    in_specs = [pl.BlockSpec((tm, k), lambda j, i: (i, 0))]
    args = [h]
    if mode != "plain" and tm != seq_len:
        per = tm // HALO
        in_specs += [pl.BlockSpec((HALO, k), lambda j, i: (jnp.maximum(i * per - 1, 0), 0)),
                     pl.BlockSpec((HALO, k), lambda j, i: (jnp.minimum((i + 1) * per, t // HALO - 1), 0))]
        args += [h, h]
    in_specs.append(pl.BlockSpec((k, tn), lambda j, i: (0, j)))
    args.append(w)
    for prm in params:
        in_specs.append(pl.BlockSpec((prm.shape[0], tn), lambda j, i: (0, j)))
        args.append(prm)
    return pl.pallas_call(
        functools.partial(_proj_kernel, mode=mode, tm=tm, seq_len=seq_len),
        grid=(n // tn, t // tm),
        in_specs=in_specs,
        out_specs=pl.BlockSpec((tm, tn), lambda j, i: (i, j)),
        out_shape=jax.ShapeDtypeStruct((t, n), F32),
        scratch_shapes=scratch,
        compiler_params=_cparams(("arbitrary", "arbitrary")),
        name="in_proj_" + mode,
    )(*args)


def _ssd_kernel(*refs, rev, has_init, has_prev):
    refs = list(refs)
    xs_ref, b_ref, c_ref, dt_ref, cs_ref, cst_ref = refs[:6]
    pos = 6
    s0_ref = None
    if has_init:
        s0_ref = refs[pos]
        pos += 1
    yprev_ref = dskip_ref = None
    if has_prev:
        yprev_ref, dskip_ref = refs[pos], refs[pos + 1]
        pos += 2
    y_ref, sfin_ref, s_ref = refs[pos], refs[pos + 1], refs[pos + 2]

    q = SSM_CHUNK
    c = pl.program_id(1)

    @pl.when(c == 0)
    def _():
        if has_init:
            s_ref[...] = s0_ref[0]
        else:
            s_ref[...] = jnp.zeros_like(s_ref)

    ii = lax.broadcasted_iota(jnp.int32, (q, q), 0)
    jj = lax.broadcasted_iota(jnp.int32, (q, q), 1)
    incl = (jj >= ii) if rev else (jj <= ii)
    lane_lo = jj < SSM_HEADDIM
    row_lo = ii < SSM_HEADDIM
    last = 0 if rev else q - 1

    dt = dt_ref[0, 0]
    cs = cs_ref[0, 0]
    cs_t = cst_ref[0, 0, 0]

    for g in range(SSM_GROUPS):
        bm = b_ref[0, :, g * SSM_STATE:(g + 1) * SSM_STATE].astype(BF16)
        cm = c_ref[0, :, g * SSM_STATE:(g + 1) * SSM_STATE].astype(BF16)
        cb = _dot_nt(cm, bm)
        ps = [g * (SSM_PAIRS // SSM_GROUPS) + jp for jp in range(SSM_PAIRS // SSM_GROUPS)]
        idx = range(len(ps))
        lanes = [slice(p * LANES, (p + 1) * LANES) for p in ps]
        xs = [xs_ref[0, :, lanes[i]] for i in idx]
        cs_pair = [jnp.where(lane_lo, cs[:, 2 * p:2 * p + 1], cs[:, 2 * p + 1:2 * p + 2]) for p in ps]
        dt_pair = [jnp.where(lane_lo, dt[:, 2 * p:2 * p + 1], dt[:, 2 * p + 1:2 * p + 2]) for p in ps]
        xdt = [xs[i] * dt_pair[i] for i in idx]
        xdt_b = [q_.astype(BF16) for q_ in xdt]
        y_heads = []
        for i, p in enumerate(ps):
            for h in (2 * p, 2 * p + 1):
                seg = cs[:, h:h + 1] - cs_t[h:h + 1, :]
                lm = jnp.where(incl, jnp.exp(seg), 0.0)
                y_heads.append(_dot((cb * lm).astype(BF16), xdt_b[i]))
        s_pair = [s_ref[p] for p in ps]
        y_off = [_dot_nt(cm, s_pair[i].astype(BF16)) * jnp.exp(cs_pair[i]) for i in idx]
        for i in idx:
            y = jnp.where(lane_lo, y_heads[2 * i], y_heads[2 * i + 1]) + y_off[i]
            if has_prev:
                y = y + yprev_ref[0, :, lanes[i]] + dskip_ref[:, lanes[i]] * xs[i]
            y_ref[0, :, lanes[i]] = y
        decay = [jnp.exp(cs_pair[i][last:last + 1, :] - cs_pair[i]) for i in idx]
        upd = [_dot((xdt[i] * decay[i]).T.astype(BF16), bm) for i in idx]
        for i, p in enumerate(ps):
            end_col = jnp.where(row_lo, cs_t[2 * p:2 * p + 1, last:last + 1], cs_t[2 * p + 1:2 * p + 2, last:last + 1])
            s_ref[p] = s_pair[i] * jnp.exp(end_col) + upd[i]

    @pl.when(c == pl.num_programs(1) - 1)
    def _():
        sfin_ref[0] = s_ref[...]


def _ssd_prep_kernel(raw_ref, alog_ref, dtb_ref, dt_ref, cs_ref, cst_ref, *, nchunks):
    q = SSM_CHUNK
    ii = lax.broadcasted_iota(jnp.int32, (q, q), 0)
    jj = lax.broadcasted_iota(jnp.int32, (q, q), 1)
    tri = [[(jj <= ii).astype(BF16)], [(jj >= ii).astype(BF16)]]
    chains = [(d, ck) for d in range(2) for ck in range(nchunks)]
    rows = [slice(ck * q, (ck + 1) * q) for _, ck in chains]
    dt = [_softplus(raw_ref[0, rows[i], :] + dtb_ref[d:d + 1, :]) for i, (d, _) in enumerate(chains)]
    a = [dt[i] * (-jnp.exp(alog_ref[d:d + 1, :])) for i, (d, _) in enumerate(chains)]
    cs = [_mm(tri[d], _pieces(a[i], 3)) for i, (d, _) in enumerate(chains)]
    for i, (d, ck) in enumerate(chains):
        dt_ref[d, 0, rows[i], :] = dt[i]
        cs_ref[d, 0, rows[i], :] = cs[i]
        cst_ref[d, 0, ck] = cs[i].T


def _ssd_prep(p_plain3, alog2, dtb2):
    b, l, _ = p_plain3.shape
    rows = min(l, 8 * SSM_CHUNK)
    nck = rows // SSM_CHUNK
    dt_blk = DT_OFF // LANES
    return pl.pallas_call(
        functools.partial(_ssd_prep_kernel, nchunks=nck),
        grid=(b, l // rows),
        in_specs=[pl.BlockSpec((1, rows, LANES), lambda i, j: (i, j, dt_blk)),
                  pl.BlockSpec((2, LANES), lambda i, j: (0, 0)),
                  pl.BlockSpec((2, LANES), lambda i, j: (0, 0))],
        out_specs=[pl.BlockSpec((2, 1, rows, LANES), lambda i, j: (0, i, j, 0)),
                   pl.BlockSpec((2, 1, rows, LANES), lambda i, j: (0, i, j, 0)),
                   pl.BlockSpec((2, 1, nck, SSM_CHUNK, LANES), lambda i, j: (0, i, j, 0, 0))],
        out_shape=[jax.ShapeDtypeStruct((2, b, l, LANES), F32),
                   jax.ShapeDtypeStruct((2, b, l, LANES), F32),
                   jax.ShapeDtypeStruct((2, b, l // SSM_CHUNK, SSM_CHUNK, LANES), F32)],
        compiler_params=_cparams(("arbitrary", "arbitrary")),
        name="ssd_prep",
    )(p_plain3, alog2, dtb2)


def _ssd_scan(xbc, prep, s0, yprev, dskip, rev):
    b, l, _ = xbc.shape
    nc = l // SSM_CHUNK
    q = SSM_CHUNK
    cidx = (lambda c: nc - 1 - c) if rev else (lambda c: c)
    d = 1 if rev else 0
    in_specs = [pl.BlockSpec((1, q, SSM_D_INNER), lambda i, c: (i, cidx(c), 0)),
                pl.BlockSpec((1, q, 512), lambda i, c: (i, cidx(c), SSM_D_INNER // 512)),
                pl.BlockSpec((1, q, 512), lambda i, c: (i, cidx(c), SSM_D_INNER // 512 + 1)),
                pl.BlockSpec((1, 1, q, LANES), lambda i, c: (d, i, cidx(c), 0)),
                pl.BlockSpec((1, 1, q, LANES), lambda i, c: (d, i, cidx(c), 0)),
                pl.BlockSpec((1, 1, 1, q, LANES), lambda i, c: (d, i, cidx(c), 0, 0))]
    args = [xbc, xbc, xbc, *prep]
    if s0 is not None:
        in_specs.append(pl.BlockSpec((1, SSM_PAIRS, LANES, SSM_STATE), lambda i, c: (i, 0, 0, 0)))
        args.append(s0)
    if yprev is not None:
        in_specs.append(pl.BlockSpec((1, q, SSM_D_INNER), lambda i, c: (i, cidx(c), 0)))
        in_specs.append(pl.BlockSpec((1, SSM_D_INNER), lambda i, c: (0, 0)))
        args += [yprev, dskip]
    kern = functools.partial(_ssd_kernel, rev=rev, has_init=s0 is not None, has_prev=yprev is not None)
    return pl.pallas_call(
        kern,
        grid=(b, nc),
        in_specs=in_specs,
        out_specs=[pl.BlockSpec((1, q, SSM_D_INNER), lambda i, c: (i, cidx(c), 0)),
                   pl.BlockSpec((1, SSM_PAIRS, LANES, SSM_STATE), lambda i, c: (i, 0, 0, 0))],
        out_shape=[jax.ShapeDtypeStruct((b, l, SSM_D_INNER), F32),
                   jax.ShapeDtypeStruct((b, SSM_PAIRS, LANES, SSM_STATE), F32)],
        scratch_shapes=[pltpu.VMEM((SSM_PAIRS, LANES, SSM_STATE), F32)],
        compiler_params=_cparams(("arbitrary", "arbitrary")),
        name="ssd_bwd" if rev else "ssd_fwd",
    )(*args)


PREC = {
    "lora": (1, 1),
    "cum": (1, 2),
    "seg": (1, 1),
    "amat": (1, 1),
    "inv": (1, 1),
    "state": (1, 1),
    "gv": (1, 1),
    "u": (1, 1),
    "o": (1, 1),
    "upd": (1, 1),
}


def _pieces(x, n):
    out = []
    for i in range(n):
        h = x.astype(BF16)
        out.append(h)
        if i + 1 < n:
            x = x - h.astype(F32)
    return out


def _mm(a_pieces, b_pieces, nt=False):
    dot = _dot_nt if nt else _dot
    depth = max(len(a_pieces), len(b_pieces))
    acc = None
    for i in reversed(range(len(a_pieces))):
        for j in reversed(range(len(b_pieces))):
            if i + j < depth:
                t = dot(a_pieces[i], b_pieces[j])
                acc = t if acc is None else acc + t
    return acc


def _block_diag(y, m_lo, m_hi):
    return jnp.concatenate([y * m_lo, y * m_hi], axis=0)


def _wkv_kernel(*refs, rev, has_init, final, grid_rows):
    refs = list(refs)
    (rw_ref, w0_ref, w2_ref, a0_ref, a2_ref, kk_ref, ka_ref) = refs[:7]
    pos = 7
    s0_ref = None
    if has_init:
        s0_ref = refs[pos]
        pos += 1
    of_ref = rk_ref = lnw_ref = lnb_ref = None
    if final:
        of_ref, rk_ref, lnw_ref, lnb_ref = refs[pos:pos + 4]
        pos += 4
    o_ref, sfin_ref, s_ref = refs[pos], refs[pos + 1], refs[pos + 2]

    n = WKV_CHUNK
    c = pl.program_id(1)

    @pl.when(c == 0)
    def _():
        if has_init:
            s_ref[...] = s0_ref[0]
        else:
            s_ref[...] = jnp.zeros_like(s_ref)

    xl = rw_ref[0, :, 3 * WKV_WIDTH:3 * WKV_WIDTH + LANES]
    a_full = _sigmoid(a0_ref[...] + _mm(_pieces(xl, PREC["lora"][0]), [a2_ref[...]]))
    xw = w0_ref[...] + _mm(_pieces(jnp.tanh(xl), PREC["lora"][0]), [w2_ref[...]])
    lw_full = -math.exp(-0.5) * _sigmoid(xw)

    ti = lax.broadcasted_iota(jnp.int32, (n, LANES), 0)
    li = lax.broadcasted_iota(jnp.int32, (n, LANES), 1)
    si = li & (WKV_HEADSIZE - 1)
    strict = (si > ti) if rev else (si < ti)
    incl = (si >= ti) if rev else (si <= ti)
    eye = (si == ti).astype(F32)
    m_lo = (li < WKV_HEADSIZE).astype(BF16)
    m_hi = (li >= WKV_HEADSIZE).astype(BF16)
    tq = lax.broadcasted_iota(jnp.int32, (n, n), 0)
    sq = lax.broadcasted_iota(jnp.int32, (n, n), 1)
    tri = [((sq >= tq) if rev else (sq <= tq)).astype(BF16)]
    r2 = lax.broadcasted_iota(jnp.int32, (LANES, LANES), 0)
    c2 = lax.broadcasted_iota(jnp.int32, (LANES, LANES), 1)
    same_head = (r2 < WKV_HEADSIZE) == (c2 < WKV_HEADSIZE)
    ones_bd = [same_head.astype(BF16)]
    last = 0 if rev else n - 1

    def level_mask(m):
        same = (ti >> (m.bit_length())) == (si >> (m.bit_length()))
        t_hi = (ti & m) != 0
        s_hi = (si & m) != 0
        if rev:
            return same & jnp.logical_not(t_hi) & s_hi
        return same & t_hi & jnp.logical_not(s_hi)

    levels = []
    m = 2
    while m < n:
        levels.append(level_mask(m).astype(BF16))
        m *= 2
    level1 = level_mask(1)

    def bd(pieces):
        return [_block_diag(q, m_lo, m_hi) for q in pieces]

    def seg_sum(x):
        return _mm(_pieces(x, PREC["seg"][0]), ones_bd)

    def lane_blk(p, base=0):
        return slice(base + p * LANES, base + (p + 1) * LANES)

    def rows(j):
        return slice(j * n, (j + 1) * n)

    subs = list(range(WKV_SUB))[::-1] if rev else list(range(WKV_SUB))
    chains = [(j, p) for j in subs for p in range(N_PAIRS)]
    idx = range(len(chains))
    r = [rw_ref[0, rows(j), lane_blk(p)] for j, p in chains]
    k = [rw_ref[0, rows(j), lane_blk(p, WKV_WIDTH)] for j, p in chains]
    v = [rw_ref[0, rows(j), lane_blk(p, 2 * WKV_WIDTH)] for j, p in chains]
    a = [a_full[rows(j), lane_blk(p)] for j, p in chains]
    lw = [lw_full[rows(j), lane_blk(p)] for j, p in chains]
    kkr = [k[i] * kk_ref[p] for i, (j, p) in enumerate(chains)]
    ss = [seg_sum(q * q) for q in kkr]
    kk = [kkr[i] / jnp.maximum(jnp.sqrt(ss[i]), 1e-12) for i in idx]
    kmod = [k[i] * (1.0 + (a[i] - 1.0) * ka_ref[p]) for i, (j, p) in enumerate(chains)]
    kka = [kk[i] * a[i] for i in idx]

    lg = [_mm(tri, _pieces(q, PREC["cum"][1])) for q in lw]
    g_inv = [jnp.exp(-q) for q in lg]
    rh = [r[i] * jnp.exp(lg[i]) for i in idx]
    kh = [kmod[i] * g_inv[i] for i in idx]
    ah = [kka[i] * g_inv[i] for i in idx]
    bh = [-kk[i] * jnp.exp(lg[i] - lw[i]) for i in idx]
    g_end = [jnp.exp(q[last:last + 1, :]) for q in lg]

    pa, pb = PREC["amat"]
    lhs = [_pieces(jnp.concatenate([bh[i], rh[i]], axis=0), max(pa, PREC["state"][0])) for i in idx]
    rhs = [[jnp.concatenate([qa, qk], axis=0) for qa, qk in zip(bd(_pieces(ah[i], pb)), bd(_pieces(kh[i], pb)))]
           for i in idx]
    aak = [_mm(lhs[i][:pa], rhs[i], nt=True) for i in idx]
    a_ab = [jnp.where(strict, q[:n, :LANES], 0.0) for q in aak]
    a_ra = [jnp.where(incl, q[n:, :LANES], 0.0) for q in aak]
    a_bk = [jnp.where(strict, q[:n, LANES:], 0.0) for q in aak]
    a_rk = [jnp.where(incl, q[n:, LANES:], 0.0) for q in aak]

    pa, pb = PREC["inv"]
    ab_pieces = [_pieces(q, pb) for q in a_ab]
    x = [eye + jnp.where(level1, q, 0.0) for q in a_ab]
    for lm in levels:
        e = [bd([q * lm for q in ab_pieces[i]]) for i in idx]
        pm = [_mm(_pieces(x[i], pa), e[i]) for i in idx]
        x = [x[i] + _mm(_pieces(pm[i], pa), bd(_pieces(x[i], pb))) for i in idx]
    pa, pb = PREC["gv"]
    gv = [_mm(_pieces(jnp.concatenate([a_bk[i], a_rk[i]], axis=0), pa), bd(_pieces(v[i], pb))) for i in idx]
    ake = [jnp.concatenate([ah[i], kh[i]], axis=0) * g_end[i] for i in idx]

    o = [None] * len(chains)
    for jj in range(WKV_SUB):
        ids = list(range(jj * N_PAIRS, (jj + 1) * N_PAIRS))
        s_bd = {i: s_ref[chains[i][1]] for i in ids}
        x0r = {i: _mm(lhs[i][:PREC["state"][0]], _pieces(s_bd[i], PREC["state"][1]), nt=True) for i in ids}
        pa, pb = PREC["u"]
        u = {i: _mm(_pieces(x[i], pa), bd(_pieces(x0r[i][:n] + gv[i][:n], pb))) for i in ids}
        pa, pb = PREC["o"]
        for i in ids:
            o[i] = x0r[i][n:] + gv[i][n:] + _mm(_pieces(a_ra[i], pa), bd(_pieces(u[i], pb)))
        pa, pb = PREC["upd"]
        uvt = {i: jnp.concatenate([u[i], v[i]], axis=0).T for i in ids}
        upd = {i: _mm(_pieces(uvt[i], pa), _pieces(ake[i], pb)) for i in ids}
        for i in ids:
            s_ref[chains[i][1]] = s_bd[i] * g_end[i] + jnp.where(same_head, upd[i], 0.0)

    if final:
        o = [o[i] + of_ref[0, p, rows(j), :] for i, (j, p) in enumerate(chains)]
        mu = [seg_sum(q) * (1.0 / WKV_HEADSIZE) for q in o]
        d = [o[i] - mu[i] for i in idx]
        var = [seg_sum(q * q) * (1.0 / WKV_HEADSIZE) for q in d]
        bonus = [seg_sum(r[i] * kmod[i] * rk_ref[p]) * v[i] for i, (j, p) in enumerate(chains)]
        o = [d[i] * lax.rsqrt(var[i] + WKV_GN_EPS) * lnw_ref[p] + lnb_ref[p] + bonus[i]
             for i, (j, p) in enumerate(chains)]
    if grid_rows is None:
        for i, (j, p) in enumerate(chains):
            o_ref[0, p, rows(j), :] = o[i]
    else:
        step = (pl.num_programs(1) - 1 - c) if rev else c
        cols_per_sub = n // grid_rows
        for i, (j, p) in enumerate(chains):
            for wl in range(cols_per_sub):
                w = (step * WKV_SUB + j) * cols_per_sub + wl
                o_ref[0, p, pl.ds(w, grid_rows, stride=GRID_W), :] = o[i][wl * grid_rows:(wl + 1) * grid_rows]

    @pl.when(c == pl.num_programs(1) - 1)
    def _():
        sfin_ref[0] = s_ref[...]


def _pairs(vec):
    return vec.reshape(N_PAIRS, 1, LANES)


def _wkv_scan(rw_s, w0, w2p, a0, a2p, k_k, k_a, s0, o_f, r_k, ln_w, ln_b, rev, to_row_major=False):
    b, l, w = rw_s.shape
    n = WKV_CHUNK * WKV_SUB
    nc = l // n
    cidx = (lambda c: nc - 1 - c) if rev else (lambda c: c)
    final = o_f is not None
    vec_spec = pl.BlockSpec((N_PAIRS, 1, LANES), lambda i, c: (0, 0, 0))
    st_spec = pl.BlockSpec((1, N_PAIRS, LANES, LANES), lambda i, c: (i, 0, 0, 0))
    in_specs = [pl.BlockSpec((1, n, w), lambda i, c: (i, cidx(c), 0)),
                pl.BlockSpec((1, WKV_WIDTH), lambda i, c: (0, 0)),
                pl.BlockSpec((LANES, WKV_WIDTH), lambda i, c: (0, 0)),
                pl.BlockSpec((1, WKV_WIDTH), lambda i, c: (0, 0)),
                pl.BlockSpec((LANES, WKV_WIDTH), lambda i, c: (0, 0)),
                vec_spec, vec_spec]
    args = [rw_s, w0.reshape(1, WKV_WIDTH), w2p, a0.reshape(1, WKV_WIDTH), a2p, _pairs(k_k), _pairs(k_a)]
    if s0 is not None:
        in_specs.append(st_spec)
        args.append(s0)
    if final:
        in_specs += [pl.BlockSpec((1, N_PAIRS, n, LANES), lambda i, c: (i, 0, cidx(c), 0)),
                     vec_spec, vec_spec, vec_spec]
        args += [o_f, _pairs(r_k), _pairs(ln_w), _pairs(ln_b)]
    o_spec = pl.BlockSpec((1, N_PAIRS, n, LANES), lambda i, c: (i, 0, cidx(c), 0))
    grid_rows = None
    if to_row_major:
        o_spec = pl.BlockSpec((1, N_PAIRS, l, LANES), lambda i, c: (i, 0, 0, 0))
        grid_rows = l // GRID_W
        assert WKV_CHUNK % grid_rows == 0
    kern = functools.partial(_wkv_kernel, rev=rev, has_init=s0 is not None, final=final, grid_rows=grid_rows)
    return pl.pallas_call(
        kern,
        grid=(b, nc),
        in_specs=in_specs,
        out_specs=[o_spec, st_spec],
        out_shape=[jax.ShapeDtypeStruct((b, N_PAIRS, l, LANES), F32),
                   jax.ShapeDtypeStruct((b, N_PAIRS, LANES, LANES), F32)],
        scratch_shapes=[pltpu.VMEM((N_PAIRS, LANES, LANES), F32)],
        compiler_params=_cparams(("arbitrary", "arbitrary")),
        name="wkv_bwd" if rev else "wkv_fwd",
    )(*args)


def _final_kernel(y_ref, za_ref, o_ref, zb_ref, ga_ref, gb_ref, x_ref, mod_ref,
                  ng_ref, pa_ref, pb_ref, wo_ref, fg_ref, out_ref):
    y = y_ref[...] * _silu(za_ref[...])
    y = y * lax.rsqrt(jnp.mean(y * y, axis=-1, keepdims=True) + NORM_EPS) * ng_ref[...]
    u_a = _dot(y.astype(BF16), pa_ref[...])
    o = jnp.concatenate([jnp.concatenate([o_ref[s, p] for p in range(N_PAIRS)], axis=1)
                         for s in range(o_ref.shape[0])], axis=0)
    u_b = _dot((o * _silu(zb_ref[...])).astype(BF16), pb_ref[...])
    m = _sigmoid(ga_ref[...]) * u_a + _sigmoid(gb_ref[...]) * u_b
    out = _dot(m.astype(BF16), wo_ref[...])
    gate = mod_ref[0][:, 2 * D_MODEL:]
    xo = x_ref[...] + gate * out
    out_ref[...] = xo * lax.rsqrt(jnp.mean(xo * xo, axis=-1, keepdims=True) + NORM_EPS) * fg_ref[...]


def _final(y2, p_plain, o4, x2, mod3, ssm_norm_g, p_a, p_b, w_out, final_g, row0, tiles_per_row, tm):
    t = x2.shape[0]
    w1 = D_MODEL
    l = o4.shape[2]
    o_rows = min(tm, l)
    tiles_per_seq = l // o_rows

    def resident(shape):
        return pl.BlockSpec(shape, lambda i: (0, 0), pipeline_mode=pl.Buffered(1))

    return pl.pallas_call(
        _final_kernel,
        grid=(t // tm,),
        in_specs=[pl.BlockSpec((tm, SSM_D_INNER), lambda i: (i, 0)),
                  pl.BlockSpec((tm, SSM_D_INNER), lambda i: (i, 0)),
                  pl.BlockSpec((tm // o_rows, N_PAIRS, o_rows, LANES),
                               lambda i: (i // tiles_per_seq, 0, i % tiles_per_seq, 0)),
                  pl.BlockSpec((tm, w1), lambda i: (i, ZB_OFF // w1)),
                  pl.BlockSpec((tm, w1), lambda i: (i, GL_OFF // w1)),
                  pl.BlockSpec((tm, w1), lambda i: (i, GL_OFF // w1 + 1)),
                  pl.BlockSpec((tm, w1), lambda i: (i, 0)),
                  pl.BlockSpec((1, 1, 3 * D_MODEL), lambda i: (row0 + i // tiles_per_row, 0, 0)),
                  resident((1, SSM_D_INNER)),
                  resident((SSM_D_INNER, D_MODEL)),
                  resident((WKV_WIDTH, D_MODEL)),
                  resident((D_MODEL, D_MODEL)),
                  resident((1, D_MODEL))],
        out_specs=pl.BlockSpec((tm, D_MODEL), lambda i: (i, 0)),
        out_shape=jax.ShapeDtypeStruct((t, D_MODEL), F32),
        compiler_params=_cparams(("arbitrary",)),
        name="merge_out",
    )(y2, p_plain, o4, p_plain, p_plain, p_plain, x2, mod3,
      ssm_norm_g.reshape(1, SSM_D_INNER), p_a, p_b, w_out, final_g.reshape(1, D_MODEL))


def _to_col_major(x):
    b, l, c = x.shape
    rows = l // GRID_W
    return x.reshape(b, rows, GRID_W, c).transpose(0, 2, 1, 3).reshape(b, l, c)


def _wkv_state_to_pairs(s):
    b = s.shape[0]
    s = s.reshape(b, N_PAIRS, 2, WKV_HEADSIZE, WKV_HEADSIZE)
    z = jnp.zeros_like(s[:, :, 0])
    top = jnp.concatenate([s[:, :, 0], z], axis=-1)
    bot = jnp.concatenate([z, s[:, :, 1]], axis=-1)
    return jnp.concatenate([top, bot], axis=-2)


def _wkv_state_from_pairs(s):
    b = s.shape[0]
    h = WKV_HEADSIZE
    return jnp.stack([s[:, :, :h, :h], s[:, :, h:, h:]], axis=2).reshape(b, WKV_HEADS, h, h)


def _group(x, mod3, row0, grid, states, wts):
    b, l, _ = x.shape
    t = b * l
    x2 = x.reshape(t, D_MODEL)
    h = _adaln(x2, mod3, wts["norm_g"], row0, (l if grid else t) // ADALN_TM, ADALN_TM)
    p_plain = _project(h, wts["w_plain"], tn=PLAIN_TN)
    p_plain3 = p_plain.reshape(b, l, PLAIN_W)
    h_rw = _to_col_major(h.reshape(b, l, D_MODEL)).reshape(t, D_MODEL) if grid else h

    xbc = _project(h, wts["w_xbc"], tn=XBC_W // 2, mode="conv", seq_len=l,
                   params=(wts["conv_w"], wts["conv_b"].reshape(1, XBC_W))).reshape(b, l, XBC_W)
    s_f = s_b = None
    if states is not None:
        s_f = states[0].reshape(b, SSM_PAIRS, LANES, SSM_STATE)
        s_b = states[1].reshape(b, SSM_PAIRS, LANES, SSM_STATE)
    prep = _ssd_prep(p_plain3, wts["alog"], wts["dtb"])
    y_f, fs_f = _ssd_scan(xbc, prep, s_f, None, None, rev=False)
    y, fs_b = _ssd_scan(xbc, prep, s_b, y_f, wts["dskip"], rev=True)

    rw_s = _project(h_rw, wts["w_rw"], tn=RWKV_SCAN_W // 5, mode="shift", seq_len=l,
                    params=(wts["shift_mu"].reshape(1, RWKV_SCAN_W),)).reshape(b, l, RWKV_SCAN_W)
    w_f = w_b = None
    if states is not None:
        w_f = _wkv_state_to_pairs(states[2])
        w_b = _wkv_state_to_pairs(states[3])
    o_f, fw_f = _wkv_scan(rw_s, wts["w0"][0], wts["w2p"][0], wts["a0"], wts["a2p"], wts["k_k"], wts["k_a"],
                          w_f, None, None, None, None, rev=False)
    o, fw_b = _wkv_scan(rw_s, wts["w0"][1], wts["w2p"][1], wts["a0"], wts["a2p"], wts["k_k"], wts["k_a"],
                        w_b, o_f, wts["r_k"], wts["ln_w"], wts["ln_b"], rev=True, to_row_major=grid)

    out = _final(y.reshape(t, SSM_D_INNER), p_plain, o, x2, mod3, wts["ssm_norm_g"],
                 wts["p_a"], wts["p_b"], wts["w_out"], wts["final_g"], row0,
                 (l if grid else t) // FINAL_TM, FINAL_TM)
    finals = (fs_f.reshape(b, SSM_HEADS, SSM_HEADDIM, SSM_STATE), fs_b.reshape(b, SSM_HEADS, SSM_HEADDIM, SSM_STATE),
              _wkv_state_from_pairs(fw_f), _wkv_state_from_pairs(fw_b))
    return out.reshape(b, l, D_MODEL), finals


def kernel(x_prompt, x_sample, state_ssm_fwd, state_ssm_bwd, state_wkv_fwd, state_wkv_bwd, c, c_ctx, w_mod, b_mod, norm_g, w_in, conv_w, conv_b, a_log, dt_bias, d_skip, ssm_norm_g, p_a, shift_mu, w0, w2, a0, a2, k_k, k_a, r_k, ln_w, ln_b, p_b, w_out, final_g):
    depth = w_mod.shape[0]
    assert depth == 1, "single-layer stack only"
    l0 = 0
    w_in0 = w_in[l0].astype(BF16)
    zpad = jnp.zeros((WKV_RANK, WKV_WIDTH), F32)
    w_plain = jnp.concatenate([w_in0[:, :ZA_END], w_in0[:, ZB_END:], w_in0[:, RW_END:ZB_END], w_in0[:, XBC_END:DT_END],
                               jnp.zeros((D_MODEL, PLAIN_W - DT_OFF - SSM_HEADS), BF16)], axis=1)
    wts = {
        "norm_g": norm_g[l0],
        "w_plain": w_plain,
        "w_xbc": w_in0[:, ZA_END:XBC_END],
        "w_rw": w_in0[:, DT_END:RW_END],
        "conv_w": conv_w[l0], "conv_b": conv_b[l0],
        "alog": jnp.pad(a_log[l0], ((0, 0), (0, LANES - SSM_HEADS))),
        "dtb": jnp.pad(dt_bias[l0], ((0, 0), (0, LANES - SSM_HEADS))),
        "dskip": jnp.repeat(d_skip[l0], SSM_HEADDIM).reshape(1, SSM_D_INNER),
        "ssm_norm_g": ssm_norm_g[l0],
        "p_a": p_a[l0].astype(BF16), "p_b": p_b[l0].astype(BF16), "w_out": w_out[l0].astype(BF16),
        "shift_mu": shift_mu[l0],
        "w0": w0[l0],
        "w2p": [jnp.concatenate([w2[l0, d], zpad], axis=0).astype(BF16) for d in range(2)],
        "a0": a0[l0],
        "a2p": jnp.concatenate([zpad, a2[l0]], axis=0).astype(BF16),
        "k_k": k_k[l0], "k_a": k_a[l0], "r_k": r_k[l0], "ln_w": ln_w[l0], "ln_b": ln_b[l0],
        "final_g": final_g,
    }
    nb = c.shape[0]
    cond8 = jnp.concatenate([c_ctx[None, :], c, jnp.zeros((8 - 1 - nb, D_MODEL), F32)], axis=0)
    mod3 = _modulation(cond8, w_mod[l0], b_mod[l0]).reshape(8, 1, 3 * D_MODEL)

    y_prompt, (sf, sb, wf, wb) = _group(x_prompt, mod3, 0, False, None, wts)
    lat_states = (state_ssm_fwd[:, l0], state_ssm_bwd[:, l0], state_wkv_fwd[:, l0], state_wkv_bwd[:, l0])
    y_sample, _ = _group(x_sample, mod3, 1, True, lat_states, wts)
    return (y_prompt, y_sample, sf[:, None], sb[:, None], wf[:, None], wb[:, None])
```

```python
import functools
import math

import jax
import jax.numpy as jnp
from jax import lax
from jax.experimental import pallas as pl
from jax.experimental.pallas import tpu as pltpu

F32 = jnp.float32
BF16 = jnp.bfloat16
HI = lax.Precision.HIGHEST

D_MODEL = 1024
GRID_W = 64
NORM_EPS = 1e-6
SSM_D_INNER = 2048
SSM_HEADDIM = 64
SSM_HEADS = 32
SSM_GROUPS = 4
SSM_STATE = 128
SSM_CHUNK = 128
WKV_WIDTH = 1024
WKV_HEADSIZE = 64
WKV_HEADS = 16
WKV_RANK = 64
WKV_GN_EPS = 64e-5
WKV_CHUNK = 64
XBC_W = SSM_D_INNER + 2 * SSM_GROUPS * SSM_STATE
RWKV_SCAN_W = 3 * WKV_WIDTH + 2 * WKV_RANK
ZA_END = SSM_D_INNER
XBC_END = ZA_END + XBC_W
DT_END = XBC_END + SSM_HEADS
RW_END = DT_END + RWKV_SCAN_W
ZB_END = RW_END + WKV_WIDTH
IN_W = ZB_END + 2 * D_MODEL

LANES = 128
N_PAIRS = WKV_HEADS // 2
SSM_PAIRS = SSM_HEADS // 2
WKV_SUB = 4
FINAL_TM = 512
ADALN_TM = 1024
CM_COLS = 8
HALO = 16
GL_OFF = SSM_D_INNER
ZB_OFF = GL_OFF + 2 * D_MODEL
DT_OFF = ZB_OFF + WKV_WIDTH
PLAIN_TN = 21 * LANES
PLAIN_W = 2 * PLAIN_TN
VMEM_LIMIT = 48 * 1024 * 1024


def _cparams(sem):
    return pltpu.CompilerParams(dimension_semantics=sem, vmem_limit_bytes=VMEM_LIMIT)


def _sigmoid(x):
    return 1.0 / (1.0 + jnp.exp2(x * (-1.0 / math.log(2.0))))


def _silu(x):
    return x * _sigmoid(x)


def _softplus(x):
    return jnp.maximum(x, 0.0) + jnp.log(1.0 + jnp.exp(-jnp.abs(x)))


def _dot(a, b, precision=None):
    return jnp.dot(a, b, preferred_element_type=F32, precision=precision)


def _dot_nt(a, b, precision=None):
    return lax.dot_general(a, b, (((1,), (1,)), ((), ())), preferred_element_type=F32, precision=precision)


def _bdot(a, b):
    return _dot(a.astype(BF16), b.astype(BF16))


def _bdot_nt(a, b):
    return _dot_nt(a.astype(BF16), b.astype(BF16))


def _mod_kernel(c_ref, w_ref, b_ref, o_ref):
    c = c_ref[...]
    o_ref[...] = _dot(_silu(c), w_ref[...], HI) + b_ref[...]


def _modulation(cond8, w_mod, b_mod):
    n = w_mod.shape[1]
    tn = 1024
    return pl.pallas_call(
        _mod_kernel,
        grid=(n // tn,),
        in_specs=[pl.BlockSpec((8, D_MODEL), lambda j: (0, 0)),
                  pl.BlockSpec((D_MODEL, tn), lambda j: (0, j)),
                  pl.BlockSpec((1, tn), lambda j: (0, j))],
        out_specs=pl.BlockSpec((8, tn), lambda j: (0, j)),
        out_shape=jax.ShapeDtypeStruct((8, n), F32),
        compiler_params=_cparams(("arbitrary",)),
        name="modulation",
    )(cond8, w_mod, b_mod.reshape(1, n))


def _h_kernel(x_ref, mod_ref, g_ref, h_ref, *, grid_rows):
    if grid_rows is None:
        x = x_ref[...]
    else:
        x = x_ref[0].reshape(grid_rows * CM_COLS, D_MODEL)
    y = x * lax.rsqrt(jnp.mean(x * x, axis=-1, keepdims=True) + NORM_EPS) * g_ref[...]
    m = mod_ref[0]
    shift = m[:, :D_MODEL]
    scale = m[:, D_MODEL:2 * D_MODEL]
    h = (y * (1.0 + scale) + shift).astype(BF16)
    if grid_rows is not None:
        n = grid_rows * CM_COLS
        dst = lax.broadcasted_iota(jnp.int32, (n, n), 0)
        src = lax.broadcasted_iota(jnp.int32, (n, n), 1)
        perm = (src == (dst % grid_rows) * CM_COLS + dst // grid_rows).astype(BF16)
        h = _dot(perm, h).astype(BF16)
    h_ref[...] = h


def _adaln_col_major(x, mod3, norm_g, row0):
    b, l, _ = x.shape
    rows = l // GRID_W
    tile = rows * CM_COLS
    return pl.pallas_call(
        functools.partial(_h_kernel, grid_rows=rows),
        grid=(b, GRID_W // CM_COLS),
        in_specs=[pl.BlockSpec((1, rows, CM_COLS, D_MODEL), lambda i, j: (i, 0, j, 0)),
                  pl.BlockSpec((1, 1, 3 * D_MODEL), lambda i, j: (row0 + i, 0, 0)),
                  pl.BlockSpec((1, D_MODEL), lambda i, j: (0, 0))],
        out_specs=pl.BlockSpec((tile, D_MODEL), lambda i, j: (i * (GRID_W // CM_COLS) + j, 0)),
        out_shape=jax.ShapeDtypeStruct((b * l, D_MODEL), BF16),
        compiler_params=_cparams(("arbitrary", "arbitrary")),
        name="adaln_norm_cm",
    )(x.reshape(b, rows, GRID_W, D_MODEL), mod3, norm_g.reshape(1, D_MODEL))


def _adaln(x2, mod3, norm_g, row0, tiles_per_row, tm):
    t = x2.shape[0]
    return pl.pallas_call(
        functools.partial(_h_kernel, grid_rows=None),
        grid=(t // tm,),
        in_specs=[pl.BlockSpec((tm, D_MODEL), lambda i: (i, 0)),
                  pl.BlockSpec((1, 1, 3 * D_MODEL), lambda i: (row0 + i // tiles_per_row, 0, 0)),
                  pl.BlockSpec((1, D_MODEL), lambda i: (0, 0))],
        out_specs=pl.BlockSpec((tm, D_MODEL), lambda i: (i, 0)),
        out_shape=jax.ShapeDtypeStruct((t, D_MODEL), BF16),
        compiler_params=_cparams(("arbitrary",)),
        name="adaln_norm",
    )(x2, mod3, norm_g.reshape(1, D_MODEL))


def _proj_kernel(*refs, mode, tm, seq_len):
    if mode == "plain":
        h_ref, w_ref, o_ref = refs
        o_ref[...] = _dot(h_ref[...], w_ref[...])
        return
    halo = tm != seq_len
    cw, rb_rows, row_chunk = EPI_BLOCK[mode]
    rc_rows = min(row_chunk, tm)
    if halo:
        h_ref, hp_ref, hn_ref, w_ref = refs[:4]
        n_cb = w_ref.shape[1] // cw
        params = refs[4:len(refs) - 1 - n_cb]
        start = pl.program_id(1) * tm
        keep_prev = jnp.where((start & (seq_len - 1)) != 0, 1.0, 0.0)
        keep_next = jnp.where(((start + tm) & (seq_len - 1)) != 0, 1.0, 0.0)
    else:
        h_ref, w_ref = refs[:2]
        n_cb = w_ref.shape[1] // cw
        params = refs[2:len(refs) - 1 - n_cb]
    o_ref = refs[len(refs) - 1 - n_cb]
    p_scrs = refs[len(refs) - n_cb:]
    pad = 8
    sub = lax.broadcasted_iota(jnp.int32, (pad, cw), 0)
    nb = rb_rows // pad
    sub3 = lax.broadcasted_iota(jnp.int32, (nb, pad, cw), 1)
    n_rc = tm // rc_rows

    def matmul_unit(cb, rc):
        cols = slice(cb * cw, (cb + 1) * cw)
        p_scr = p_scrs[cb]
        rows = slice(rc * rc_rows, (rc + 1) * rc_rows)
        if rc == 0 and halo:
            lhs = jnp.concatenate([h_ref[rows, :], hp_ref[...], hn_ref[...]], axis=0)
        else:
            lhs = h_ref[rows, :]
        p = _dot(lhs, w_ref[:, cols])
        p_scr[pad + rc * rc_rows:pad + (rc + 1) * rc_rows, :] = p[:rc_rows]
        if rc == 0 and halo:
            prow = p[rc_rows + HALO - 1:rc_rows + HALO] * keep_prev
            nrow = p[rc_rows + HALO:rc_rows + HALO + 1] * keep_next
            p_scr[0:pad, :] = jnp.where(sub == pad - 1, prow, 0.0)
            p_scr[pad + tm:2 * pad + tm, :] = jnp.where(sub == 0, nrow, 0.0)
        elif rc == 0:
            p_scr[0:pad, :] = jnp.zeros((pad, cw), F32)
            p_scr[pad + tm:2 * pad + tm, :] = jnp.zeros((pad, cw), F32)

    def epilogue_unit(cb, rc):
        cols = slice(cb * cw, (cb + 1) * cw)
        per = rc_rows // rb_rows
        for rb in range(rc * per, (rc + 1) * per):
            r0 = pad + rb * rb_rows
            rs = slice(rb * rb_rows, (rb + 1) * rb_rows)
            win = p_scrs[cb][r0 - pad:r0 + rb_rows + pad, :].reshape(nb + 2, pad, cw)
            cur = win[1:nb + 1]
            down = pltpu.roll(win[:nb + 1], 1, 1)
            up = pltpu.roll(win[1:], pad - 1, 1)
            prev = jnp.where(sub3 == 0, down[:nb], down[1:])
            nxt = jnp.where(sub3 == pad - 1, up[1:], up[:nb])
            if mode == "conv":
                cw_ref, cb_ref = params
                out = _silu(cw_ref[0:1, cols] * prev + cw_ref[1:2, cols] * cur
                            + cw_ref[2:3, cols] * nxt + cb_ref[:, cols])
            else:
                (mu_ref,) = params
                out = cur + mu_ref[:, cols] * (0.5 * (prev + nxt) - cur)
            o_ref[rs, cols] = out.reshape(rb_rows, cw)

    units = [(cb, rc) for cb in range(n_cb) for rc in range(n_rc)]
    matmul_unit(*units[0])
    for i, unit in enumerate(units):
        if i + 1 < len(units):
            matmul_unit(*units[i + 1])
        epilogue_unit(*unit)


EPI_BLOCK = {"conv": (256, 32, 128), "shift": (640, 16, 512)}


def _project(h, w, tn, mode="plain", seq_len=None, params=(), tm=512):
    t, k = h.shape
    n = w.shape[1]
    scratch = []
    if mode != "plain":
        tm = min(tm, seq_len)
        assert seq_len & (seq_len - 1) == 0 and seq_len % tm == 0 and tm % HALO == 0
        cw = EPI_BLOCK[mode][0]
        scratch = [pltpu.VMEM((tm + 16, cw), F32)] * (tn // cw)
    in_specs = [pl.BlockSpec((tm, k), lambda j, i: (i, 0))]
    args = [h]
    if mode != "plain" and tm != seq_len:
        per = tm // HALO
        in_specs += [pl.BlockSpec((HALO, k), lambda j, i: (jnp.maximum(i * per - 1, 0), 0)),
                     pl.BlockSpec((HALO, k), lambda j, i: (jnp.minimum((i + 1) * per, t // HALO - 1), 0))]
        args += [h, h]
    w_mode = pl.Buffered(1) if tn == n else None
    in_specs.append(pl.BlockSpec((k, tn), lambda j, i: (0, j), pipeline_mode=w_mode))
    args.append(w)
    for prm in params:
        in_specs.append(pl.BlockSpec((prm.shape[0], tn), lambda j, i: (0, j)))
        args.append(prm)
    return pl.pallas_call(
        functools.partial(_proj_kernel, mode=mode, tm=tm, seq_len=seq_len),
        grid=(n // tn, t // tm),
        in_specs=in_specs,
        out_specs=pl.BlockSpec((tm, tn), lambda j, i: (i, j)),
        out_shape=jax.ShapeDtypeStruct((t, n), F32),
        scratch_shapes=scratch,
        compiler_params=_cparams(("arbitrary", "arbitrary")),
        name="in_proj_" + mode,
    )(*args)


def _ssd_kernel(*refs, rev, has_init, has_prev):
    refs = list(refs)
    xs_ref, b_ref, c_ref, cs_ref, cst_ref, dtt_ref, ddt_ref = refs[:7]
    pos = 7
    s0_ref = None
    if has_init:
        s0_ref = refs[pos]
        pos += 1
    yprev_ref = dskip_ref = None
    if has_prev:
        yprev_ref, dskip_ref = refs[pos], refs[pos + 1]
        pos += 2
    y_ref, sfin_ref, s_ref = refs[pos], refs[pos + 1], refs[pos + 2]

    q = SSM_CHUNK
    c = pl.program_id(1)

    @pl.when(c == 0)
    def _():
        if has_init:
            s_ref[...] = s0_ref[0]
        else:
            s_ref[...] = jnp.zeros_like(s_ref)

    ii = lax.broadcasted_iota(jnp.int32, (q, q), 0)
    jj = lax.broadcasted_iota(jnp.int32, (q, q), 1)
    incl = (jj >= ii) if rev else (jj <= ii)
    lane_lo = jj < SSM_HEADDIM
    row_lo = ii < SSM_HEADDIM
    last = 0 if rev else q - 1

    cs = cs_ref[0, 0]
    cs_t = cst_ref[0, 0, 0]
    dt_t = dtt_ref[0, 0, 0]
    dd_t = ddt_ref[0, 0, 0]
    m_lo = lane_lo.astype(BF16)
    m_hi = jnp.logical_not(lane_lo).astype(BF16)

    for g in range(SSM_GROUPS):
        bm = b_ref[0, :, g * SSM_STATE:(g + 1) * SSM_STATE].astype(BF16)
        cm = c_ref[0, :, g * SSM_STATE:(g + 1) * SSM_STATE].astype(BF16)
        cb = _dot_nt(cm, bm)
        ps = [g * (SSM_PAIRS // SSM_GROUPS) + jp for jp in range(SSM_PAIRS // SSM_GROUPS)]
        idx = range(len(ps))
        lanes = [slice(p * LANES, (p + 1) * LANES) for p in ps]
        xs = [xs_ref[0, :, lanes[i]] for i in idx]
        cs_col = {h: cs[:, h:h + 1] for p in ps for h in (2 * p, 2 * p + 1)}
        cs_pair = [jnp.where(lane_lo, cs_col[2 * p], cs_col[2 * p + 1]) for p in ps]
        w_pair = []
        for p in ps:
            w_heads = []
            for h in (2 * p, 2 * p + 1):
                seg = cs_col[h] - cs_t[h:h + 1, :]
                lm = jnp.where(incl, jnp.exp(seg), 0.0)
                w_heads.append((cb * lm * dt_t[h:h + 1, :]).astype(BF16))
            w_pair.append(jnp.concatenate(w_heads, axis=1))
        xs_b = [q_.astype(BF16) for q_ in xs]
        y_diag = [_dot(w_pair[i], jnp.concatenate([xs_b[i] * m_lo, xs_b[i] * m_hi], axis=0)) for i in idx]
        s_pair = [s_ref[p] for p in ps]
        y_off = [_dot_nt(cm, s_pair[i].astype(BF16)) * jnp.exp(cs_pair[i]) for i in idx]
        for i in idx:
            y = y_diag[i] + y_off[i]
            if has_prev:
                y = y + yprev_ref[0, :, lanes[i]] + dskip_ref[:, lanes[i]] * xs[i]
            y_ref[0, :, lanes[i]] = y
        scale = [jnp.where(row_lo, dd_t[2 * p:2 * p + 1, :], dd_t[2 * p + 1:2 * p + 2, :]) for p in ps]
        upd = [_dot((xs[i].T * scale[i]).astype(BF16), bm) for i in idx]
        for i, p in enumerate(ps):
            end_col = jnp.where(row_lo, cs_t[2 * p:2 * p + 1, last:last + 1], cs_t[2 * p + 1:2 * p + 2, last:last + 1])
            s_ref[p] = s_pair[i] * jnp.exp(end_col) + upd[i]

    @pl.when(c == pl.num_programs(1) - 1)
    def _():
        sfin_ref[0] = s_ref[...]


def _ssd_prep_kernel(raw_ref, alog_ref, dtb_ref, cs_ref, cst_ref, dtt_ref, ddt_ref, *, nchunks):
    q = SSM_CHUNK
    ii = lax.broadcasted_iota(jnp.int32, (q, q), 0)
    jj = lax.broadcasted_iota(jnp.int32, (q, q), 1)
    tri = [[(jj <= ii).astype(BF16)], [(jj >= ii).astype(BF16)]]
    last = [q - 1, 0]
    chains = [(d, ck) for d in range(2) for ck in range(nchunks)]
    rows = [slice(ck * q, (ck + 1) * q) for _, ck in chains]
    dt = [_softplus(raw_ref[0, rows[i], :] + dtb_ref[d:d + 1, :]) for i, (d, _) in enumerate(chains)]
    a = [dt[i] * (-jnp.exp(alog_ref[d:d + 1, :])) for i, (d, _) in enumerate(chains)]
    cs = [_mm(tri[d], _pieces(a[i], 3)) for i, (d, _) in enumerate(chains)]
    dd = [dt[i] * jnp.exp(cs[i][last[d]:last[d] + 1, :] - cs[i]) for i, (d, _) in enumerate(chains)]
    for i, (d, ck) in enumerate(chains):
        cs_ref[d, 0, rows[i], :] = cs[i]
        cst_ref[d, 0, ck] = cs[i].T
        dtt_ref[d, 0, ck] = dt[i].T
        ddt_ref[d, 0, ck] = dd[i].T


def _ssd_prep(p_plain3, alog2, dtb2):
    b, l, _ = p_plain3.shape
    rows = min(l, 8 * SSM_CHUNK)
    nck = rows // SSM_CHUNK
    dt_blk = DT_OFF // LANES
    t_spec = pl.BlockSpec((2, 1, nck, SSM_CHUNK, LANES), lambda i, j: (0, i, j, 0, 0))
    t_shape = jax.ShapeDtypeStruct((2, b, l // SSM_CHUNK, SSM_CHUNK, LANES), F32)
    return pl.pallas_call(
        functools.partial(_ssd_prep_kernel, nchunks=nck),
        grid=(b, l // rows),
        in_specs=[pl.BlockSpec((1, rows, LANES), lambda i, j: (i, j, dt_blk)),
                  pl.BlockSpec((2, LANES), lambda i, j: (0, 0)),
                  pl.BlockSpec((2, LANES), lambda i, j: (0, 0))],
        out_specs=[pl.BlockSpec((2, 1, rows, LANES), lambda i, j: (0, i, j, 0)), t_spec, t_spec, t_spec],
        out_shape=[jax.ShapeDtypeStruct((2, b, l, LANES), F32), t_shape, t_shape, t_shape],
        compiler_params=_cparams(("arbitrary", "arbitrary")),
        name="ssd_prep",
    )(p_plain3, alog2, dtb2)


def _ssd_scan(xbc, prep, s0, yprev, dskip, rev):
    b, l, _ = xbc.shape
    nc = l // SSM_CHUNK
    q = SSM_CHUNK
    cidx = (lambda c: nc - 1 - c) if rev else (lambda c: c)
    d = 1 if rev else 0
    in_specs = [pl.BlockSpec((1, q, SSM_D_INNER), lambda i, c: (i, cidx(c), 0)),
                pl.BlockSpec((1, q, 512), lambda i, c: (i, cidx(c), SSM_D_INNER // 512)),
                pl.BlockSpec((1, q, 512), lambda i, c: (i, cidx(c), SSM_D_INNER // 512 + 1)),
                pl.BlockSpec((1, 1, q, LANES), lambda i, c: (d, i, cidx(c), 0)),
                pl.BlockSpec((1, 1, 1, q, LANES), lambda i, c: (d, i, cidx(c), 0, 0)),
                pl.BlockSpec((1, 1, 1, q, LANES), lambda i, c: (d, i, cidx(c), 0, 0)),
                pl.BlockSpec((1, 1, 1, q, LANES), lambda i, c: (d, i, cidx(c), 0, 0))]
    args = [xbc, xbc, xbc, *prep]
    if s0 is not None:
        in_specs.append(pl.BlockSpec((1, SSM_PAIRS, LANES, SSM_STATE), lambda i, c: (i, 0, 0, 0)))
        args.append(s0)
    if yprev is not None:
        in_specs.append(pl.BlockSpec((1, q, SSM_D_INNER), lambda i, c: (i, cidx(c), 0)))
        in_specs.append(pl.BlockSpec((1, SSM_D_INNER), lambda i, c: (0, 0)))
        args += [yprev, dskip]
    kern = functools.partial(_ssd_kernel, rev=rev, has_init=s0 is not None, has_prev=yprev is not None)
    return pl.pallas_call(
        kern,
        grid=(b, nc),
        in_specs=in_specs,
        out_specs=[pl.BlockSpec((1, q, SSM_D_INNER), lambda i, c: (i, cidx(c), 0)),
                   pl.BlockSpec((1, SSM_PAIRS, LANES, SSM_STATE), lambda i, c: (i, 0, 0, 0))],
        out_shape=[jax.ShapeDtypeStruct((b, l, SSM_D_INNER), F32),
                   jax.ShapeDtypeStruct((b, SSM_PAIRS, LANES, SSM_STATE), F32)],
        scratch_shapes=[pltpu.VMEM((SSM_PAIRS, LANES, SSM_STATE), F32)],
        compiler_params=_cparams(("arbitrary", "arbitrary")),
        name="ssd_bwd" if rev else "ssd_fwd",
    )(*args)


CUM_PIECES = 2


def _pieces(x, n):
    out = []
    for i in range(n):
        h = x.astype(BF16)
        out.append(h)
        if i + 1 < n:
            x = x - h.astype(F32)
    return out


def _mm(a_pieces, b_pieces, nt=False):
    dot = _dot_nt if nt else _dot
    depth = max(len(a_pieces), len(b_pieces))
    acc = None
    for i in reversed(range(len(a_pieces))):
        for j in reversed(range(len(b_pieces))):
            if i + j < depth:
                t = dot(a_pieces[i], b_pieces[j])
                acc = t if acc is None else acc + t
    return acc


def _block_diag(y, m_lo, m_hi):
    return jnp.concatenate([y * m_lo, y * m_hi], axis=0)


def _wkv_kernel(*refs, rev, has_init, final, grid_rows):
    refs = list(refs)
    (rw_ref, w0_ref, w2_ref, a0_ref, a2_ref, kk_ref, ka_ref) = refs[:7]
    pos = 7
    s0_ref = None
    if has_init:
        s0_ref = refs[pos]
        pos += 1
    of_ref = rk_ref = lnw_ref = lnb_ref = None
    if final:
        of_ref, rk_ref, lnw_ref, lnb_ref = refs[pos:pos + 4]
        pos += 4
    o_ref, sfin_ref, s_ref = refs[pos], refs[pos + 1], refs[pos + 2]

    n = WKV_CHUNK
    c = pl.program_id(1)

    @pl.when(c == 0)
    def _():
        if has_init:
            s_ref[...] = s0_ref[0]
        else:
            s_ref[...] = jnp.zeros_like(s_ref)

    xl = rw_ref[0, :, 3 * WKV_WIDTH:3 * WKV_WIDTH + LANES]
    a_full = _sigmoid(a0_ref[...] + _dot(xl.astype(BF16), a2_ref[...]))
    xw = w0_ref[...] + _dot(jnp.tanh(xl).astype(BF16), w2_ref[...])
    lw_full = -math.exp(-0.5) * _sigmoid(xw)

    ti = lax.broadcasted_iota(jnp.int32, (n, LANES), 0)
    li = lax.broadcasted_iota(jnp.int32, (n, LANES), 1)
    si = li & (WKV_HEADSIZE - 1)
    strict = (si > ti) if rev else (si < ti)
    incl = (si >= ti) if rev else (si <= ti)
    eye = (si == ti).astype(BF16)
    m_lo = (li < WKV_HEADSIZE).astype(BF16)
    m_hi = (li >= WKV_HEADSIZE).astype(BF16)
    tq = lax.broadcasted_iota(jnp.int32, (n, n), 0)
    sq = lax.broadcasted_iota(jnp.int32, (n, n), 1)
    tri = [((sq >= tq) if rev else (sq <= tq)).astype(BF16)]
    r2 = lax.broadcasted_iota(jnp.int32, (LANES, LANES), 0)
    c2 = lax.broadcasted_iota(jnp.int32, (LANES, LANES), 1)
    same_head = (r2 < WKV_HEADSIZE) == (c2 < WKV_HEADSIZE)
    ones_bd = same_head.astype(BF16)
    last = 0 if rev else n - 1

    def level_mask(m):
        same = (ti >> (m.bit_length())) == (si >> (m.bit_length()))
        t_hi = (ti & m) != 0
        s_hi = (si & m) != 0
        if rev:
            return same & jnp.logical_not(t_hi) & s_hi
        return same & t_hi & jnp.logical_not(s_hi)

    levels = []
    m = 2
    while m < n:
        levels.append(level_mask(m).astype(BF16))
        m *= 2
    level1 = level_mask(1).astype(BF16)

    def bd(y):
        return _block_diag(y.astype(BF16), m_lo, m_hi)

    def seg_sum(x):
        return _dot(x.astype(BF16), ones_bd)

    def lane_blk(p, base=0):
        return slice(base + p * LANES, base + (p + 1) * LANES)

    def rows(j):
        return slice(j * n, (j + 1) * n)

    subs = list(range(WKV_SUB))[::-1] if rev else list(range(WKV_SUB))
    chains = [(j, p) for j in subs for p in range(N_PAIRS)]
    idx = range(len(chains))
    r = [rw_ref[0, rows(j), lane_blk(p)] for j, p in chains]
    k = [rw_ref[0, rows(j), lane_blk(p, WKV_WIDTH)] for j, p in chains]
    v = [rw_ref[0, rows(j), lane_blk(p, 2 * WKV_WIDTH)] for j, p in chains]
    a = [a_full[rows(j), lane_blk(p)] for j, p in chains]
    lw = [lw_full[rows(j), lane_blk(p)] for j, p in chains]
    kkr = [k[i] * kk_ref[p] for i, (j, p) in enumerate(chains)]
    ss = [seg_sum(q * q) for q in kkr]
    kk = [kkr[i] / jnp.maximum(jnp.sqrt(ss[i]), 1e-12) for i in idx]
    kmod = [k[i] * (1.0 + (a[i] - 1.0) * ka_ref[p]) for i, (j, p) in enumerate(chains)]
    kka = [kk[i] * a[i] for i in idx]

    lg = [_mm(tri, _pieces(q, CUM_PIECES)) for q in lw]
    g_inv = [jnp.exp(-q) for q in lg]
    rh = [r[i] * jnp.exp(lg[i]) for i in idx]
    kh = [kmod[i] * g_inv[i] for i in idx]
    ah = [kka[i] * g_inv[i] for i in idx]
    bh = [-kk[i] * jnp.exp(lg[i] - lw[i]) for i in idx]
    g_end = [jnp.exp(q[last:last + 1, :]) for q in lg]

    lhs = [jnp.concatenate([bh[i], rh[i]], axis=0).astype(BF16) for i in idx]
    rhs = [jnp.concatenate([bd(ah[i]), bd(kh[i])], axis=0) for i in idx]
    aak = [_dot_nt(lhs[i], rhs[i]) for i in idx]
    a_ab = [jnp.where(strict, q[:n, :LANES], 0.0).astype(BF16) for q in aak]
    a_ra = [jnp.where(incl, q[n:, :LANES], 0.0).astype(BF16) for q in aak]
    a_bkrk = [jnp.concatenate([jnp.where(strict, q[:n, LANES:], 0.0), jnp.where(incl, q[n:, LANES:], 0.0)],
                              axis=0).astype(BF16) for q in aak]

    x = [eye + q * level1 for q in a_ab]
    for lm in levels:
        pm = [_dot(x[i], bd(a_ab[i] * lm)).astype(BF16) for i in idx]
        x = [x[i] + _dot(pm[i], bd(x[i])).astype(BF16) for i in idx]
    gv = [_dot(a_bkrk[i], bd(v[i])) for i in idx]
    ake = [(jnp.concatenate([ah[i], kh[i]], axis=0) * g_end[i]).astype(BF16) for i in idx]

    o = [None] * len(chains)
    for jj in range(WKV_SUB):
        ids = list(range(jj * N_PAIRS, (jj + 1) * N_PAIRS))
        s_bd = {i: s_ref[chains[i][1]] for i in ids}
        x0r = {i: _dot_nt(lhs[i], s_bd[i].astype(BF16)) for i in ids}
        u = {i: _dot(x[i], bd(x0r[i][:n] + gv[i][:n])) for i in ids}
        for i in ids:
            o[i] = x0r[i][n:] + gv[i][n:] + _dot(a_ra[i], bd(u[i]))
        uvt = {i: jnp.concatenate([u[i], v[i]], axis=0).T.astype(BF16) for i in ids}
        upd = {i: _dot(uvt[i], ake[i]) for i in ids}
        for i in ids:
            s_ref[chains[i][1]] = s_bd[i] * g_end[i] + jnp.where(same_head, upd[i], 0.0)

    if final:
        o = [o[i] + of_ref[0, p, rows(j), :] for i, (j, p) in enumerate(chains)]
        mu = [seg_sum(q) * (1.0 / WKV_HEADSIZE) for q in o]
        d = [o[i] - mu[i] for i in idx]
        var = [seg_sum(q * q) * (1.0 / WKV_HEADSIZE) for q in d]
        bonus = [seg_sum(r[i] * kmod[i] * rk_ref[p]) * v[i] for i, (j, p) in enumerate(chains)]
        o = [d[i] * lax.rsqrt(var[i] + WKV_GN_EPS) * lnw_ref[p] + lnb_ref[p] + bonus[i]
             for i, (j, p) in enumerate(chains)]
    if grid_rows is None:
        for i, (j, p) in enumerate(chains):
            o_ref[0, p, rows(j), :] = o[i]
    else:
        step = (pl.num_programs(1) - 1 - c) if rev else c
        cols_per_sub = n // grid_rows
        for i, (j, p) in enumerate(chains):
            for wl in range(cols_per_sub):
                w = (step * WKV_SUB + j) * cols_per_sub + wl
                o_ref[0, p, pl.ds(w, grid_rows, stride=GRID_W), :] = o[i][wl * grid_rows:(wl + 1) * grid_rows]

    @pl.when(c == pl.num_programs(1) - 1)
    def _():
        sfin_ref[0] = s_ref[...]


def _pairs(vec):
    return vec.reshape(N_PAIRS, 1, LANES)


def _wkv_scan(rw_s, w0, w2p, a0, a2p, k_k, k_a, s0, o_f, r_k, ln_w, ln_b, rev, to_row_major=False):
    b, l, w = rw_s.shape
    n = WKV_CHUNK * WKV_SUB
    nc = l // n
    cidx = (lambda c: nc - 1 - c) if rev else (lambda c: c)
    final = o_f is not None
    vec_spec = pl.BlockSpec((N_PAIRS, 1, LANES), lambda i, c: (0, 0, 0))
    st_spec = pl.BlockSpec((1, N_PAIRS, LANES, LANES), lambda i, c: (i, 0, 0, 0))
    in_specs = [pl.BlockSpec((1, n, w), lambda i, c: (i, cidx(c), 0)),
                pl.BlockSpec((1, WKV_WIDTH), lambda i, c: (0, 0)),
                pl.BlockSpec((LANES, WKV_WIDTH), lambda i, c: (0, 0)),
                pl.BlockSpec((1, WKV_WIDTH), lambda i, c: (0, 0)),
                pl.BlockSpec((LANES, WKV_WIDTH), lambda i, c: (0, 0)),
                vec_spec, vec_spec]
    args = [rw_s, w0.reshape(1, WKV_WIDTH), w2p, a0.reshape(1, WKV_WIDTH), a2p, _pairs(k_k), _pairs(k_a)]
    if s0 is not None:
        in_specs.append(st_spec)
        args.append(s0)
    if final:
        in_specs += [pl.BlockSpec((1, N_PAIRS, n, LANES), lambda i, c: (i, 0, cidx(c), 0)),
                     vec_spec, vec_spec, vec_spec]
        args += [o_f, _pairs(r_k), _pairs(ln_w), _pairs(ln_b)]
    o_spec = pl.BlockSpec((1, N_PAIRS, n, LANES), lambda i, c: (i, 0, cidx(c), 0))
    grid_rows = None
    if to_row_major:
        o_spec = pl.BlockSpec((1, N_PAIRS, l, LANES), lambda i, c: (i, 0, 0, 0))
        grid_rows = l // GRID_W
        assert WKV_CHUNK % grid_rows == 0
    kern = functools.partial(_wkv_kernel, rev=rev, has_init=s0 is not None, final=final, grid_rows=grid_rows)
    return pl.pallas_call(
        kern,
        grid=(b, nc),
        in_specs=in_specs,
        out_specs=[o_spec, st_spec],
        out_shape=[jax.ShapeDtypeStruct((b, N_PAIRS, l, LANES), F32),
                   jax.ShapeDtypeStruct((b, N_PAIRS, LANES, LANES), F32)],
        scratch_shapes=[pltpu.VMEM((N_PAIRS, LANES, LANES), F32)],
        compiler_params=_cparams(("arbitrary", "arbitrary")),
        name="wkv_bwd" if rev else "wkv_fwd",
    )(*args)


def _final_kernel(y_ref, za_ref, o_ref, zb_ref, ga_ref, gb_ref, x_ref, mod_ref,
                  ng_ref, pa_ref, pb_ref, wo_ref, fg_ref, out_ref):
    y = y_ref[...] * _silu(za_ref[...])
    y = y * lax.rsqrt(jnp.mean(y * y, axis=-1, keepdims=True) + NORM_EPS) * ng_ref[...]
    u_a = _dot(y.astype(BF16), pa_ref[...])
    o = jnp.concatenate([jnp.concatenate([o_ref[s, p] for p in range(N_PAIRS)], axis=1)
                         for s in range(o_ref.shape[0])], axis=0)
    u_b = _dot((o * _silu(zb_ref[...])).astype(BF16), pb_ref[...])
    m = _sigmoid(ga_ref[...]) * u_a + _sigmoid(gb_ref[...]) * u_b
    out = _dot(m.astype(BF16), wo_ref[...])
    gate = mod_ref[0][:, 2 * D_MODEL:]
    xo = x_ref[...] + gate * out
    out_ref[...] = xo * lax.rsqrt(jnp.mean(xo * xo, axis=-1, keepdims=True) + NORM_EPS) * fg_ref[...]


def _final(y2, p_plain, o4, x2, mod3, ssm_norm_g, p_a, p_b, w_out, final_g, row0, tiles_per_row, tm):
    t = x2.shape[0]
    w1 = D_MODEL
    l = o4.shape[2]
    o_rows = min(tm, l)
    tiles_per_seq = l // o_rows

    def resident(shape):
        return pl.BlockSpec(shape, lambda i: (0, 0), pipeline_mode=pl.Buffered(1))

    return pl.pallas_call(
        _final_kernel,
        grid=(t // tm,),
        in_specs=[pl.BlockSpec((tm, SSM_D_INNER), lambda i: (i, 0)),
                  pl.BlockSpec((tm, SSM_D_INNER), lambda i: (i, 0)),
                  pl.BlockSpec((tm // o_rows, N_PAIRS, o_rows, LANES),
                               lambda i: (i // tiles_per_seq, 0, i % tiles_per_seq, 0)),
                  pl.BlockSpec((tm, w1), lambda i: (i, ZB_OFF // w1)),
                  pl.BlockSpec((tm, w1), lambda i: (i, GL_OFF // w1)),
                  pl.BlockSpec((tm, w1), lambda i: (i, GL_OFF // w1 + 1)),
                  pl.BlockSpec((tm, w1), lambda i: (i, 0)),
                  pl.BlockSpec((1, 1, 3 * D_MODEL), lambda i: (row0 + i // tiles_per_row, 0, 0)),
                  resident((1, SSM_D_INNER)),
                  resident((SSM_D_INNER, D_MODEL)),
                  resident((WKV_WIDTH, D_MODEL)),
                  resident((D_MODEL, D_MODEL)),
                  resident((1, D_MODEL))],
        out_specs=pl.BlockSpec((tm, D_MODEL), lambda i: (i, 0)),
        out_shape=jax.ShapeDtypeStruct((t, D_MODEL), F32),
        compiler_params=_cparams(("arbitrary",)),
        name="merge_out",
    )(y2, p_plain, o4, p_plain, p_plain, p_plain, x2, mod3,
      ssm_norm_g.reshape(1, SSM_D_INNER), p_a, p_b, w_out, final_g.reshape(1, D_MODEL))


def _wkv_state_to_pairs(s):
    b = s.shape[0]
    s = s.reshape(b, N_PAIRS, 2, WKV_HEADSIZE, WKV_HEADSIZE)
    z = jnp.zeros_like(s[:, :, 0])
    top = jnp.concatenate([s[:, :, 0], z], axis=-1)
    bot = jnp.concatenate([z, s[:, :, 1]], axis=-1)
    return jnp.concatenate([top, bot], axis=-2)


def _wkv_state_from_pairs(s):
    b = s.shape[0]
    h = WKV_HEADSIZE
    return jnp.stack([s[:, :, :h, :h], s[:, :, h:, h:]], axis=2).reshape(b, WKV_HEADS, h, h)


def _group(x, mod3, row0, grid, states, wts):
    b, l, _ = x.shape
    t = b * l
    x2 = x.reshape(t, D_MODEL)
    h = _adaln(x2, mod3, wts["norm_g"], row0, (l if grid else t) // ADALN_TM, ADALN_TM)
    p_plain = _project(h, wts["w_plain"], tn=PLAIN_TN)
    p_plain3 = p_plain.reshape(b, l, PLAIN_W)
    h_rw = _adaln_col_major(x, mod3, wts["norm_g"], row0) if grid else h

    xbc = _project(h, wts["w_xbc"], tn=XBC_W, mode="conv", seq_len=l,
                   params=(wts["conv_w"], wts["conv_b"].reshape(1, XBC_W))).reshape(b, l, XBC_W)
    s_f = s_b = None
    if states is not None:
        s_f = states[0].reshape(b, SSM_PAIRS, LANES, SSM_STATE)
        s_b = states[1].reshape(b, SSM_PAIRS, LANES, SSM_STATE)
    prep = _ssd_prep(p_plain3, wts["alog"], wts["dtb"])
    y_f, fs_f = _ssd_scan(xbc, prep, s_f, None, None, rev=False)
    y, fs_b = _ssd_scan(xbc, prep, s_b, y_f, wts["dskip"], rev=True)

    rw_s = _project(h_rw, wts["w_rw"], tn=RWKV_SCAN_W, mode="shift", seq_len=l,
                    params=(wts["shift_mu"].reshape(1, RWKV_SCAN_W),)).reshape(b, l, RWKV_SCAN_W)
    w_f = w_b = None
    if states is not None:
        w_f = _wkv_state_to_pairs(states[2])
        w_b = _wkv_state_to_pairs(states[3])
    o_f, fw_f = _wkv_scan(rw_s, wts["w0"][0], wts["w2p"][0], wts["a0"], wts["a2p"], wts["k_k"], wts["k_a"],
                          w_f, None, None, None, None, rev=False)
    o, fw_b = _wkv_scan(rw_s, wts["w0"][1], wts["w2p"][1], wts["a0"], wts["a2p"], wts["k_k"], wts["k_a"],
                        w_b, o_f, wts["r_k"], wts["ln_w"], wts["ln_b"], rev=True, to_row_major=grid)

    out = _final(y.reshape(t, SSM_D_INNER), p_plain, o, x2, mod3, wts["ssm_norm_g"],
                 wts["p_a"], wts["p_b"], wts["w_out"], wts["final_g"], row0,
                 (l if grid else t) // FINAL_TM, FINAL_TM)
    finals = (fs_f.reshape(b, SSM_HEADS, SSM_HEADDIM, SSM_STATE), fs_b.reshape(b, SSM_HEADS, SSM_HEADDIM, SSM_STATE),
              _wkv_state_from_pairs(fw_f), _wkv_state_from_pairs(fw_b))
    return out.reshape(b, l, D_MODEL), finals


def kernel(x_prompt, x_sample, state_ssm_fwd, state_ssm_bwd, state_wkv_fwd, state_wkv_bwd, c, c_ctx, w_mod, b_mod, norm_g, w_in, conv_w, conv_b, a_log, dt_bias, d_skip, ssm_norm_g, p_a, shift_mu, w0, w2, a0, a2, k_k, k_a, r_k, ln_w, ln_b, p_b, w_out, final_g):
    depth = w_mod.shape[0]
    assert depth == 1, "single-layer stack only"
    l0 = 0
    w_in0 = w_in[l0].astype(BF16)
    zpad = jnp.zeros((WKV_RANK, WKV_WIDTH), F32)
    w_plain = jnp.concatenate([w_in0[:, :ZA_END], w_in0[:, ZB_END:], w_in0[:, RW_END:ZB_END], w_in0[:, XBC_END:DT_END],
                               jnp.zeros((D_MODEL, PLAIN_W - DT_OFF - SSM_HEADS), BF16)], axis=1)
    wts = {
        "norm_g": norm_g[l0],
        "w_plain": w_plain,
        "w_xbc": w_in0[:, ZA_END:XBC_END],
        "w_rw": w_in0[:, DT_END:RW_END],
        "conv_w": conv_w[l0], "conv_b": conv_b[l0],
        "alog": jnp.pad(a_log[l0], ((0, 0), (0, LANES - SSM_HEADS))),
        "dtb": jnp.pad(dt_bias[l0], ((0, 0), (0, LANES - SSM_HEADS))),
        "dskip": jnp.repeat(d_skip[l0], SSM_HEADDIM).reshape(1, SSM_D_INNER),
        "ssm_norm_g": ssm_norm_g[l0],
        "p_a": p_a[l0].astype(BF16), "p_b": p_b[l0].astype(BF16), "w_out": w_out[l0].astype(BF16),
        "shift_mu": shift_mu[l0],
        "w0": w0[l0],
        "w2p": [jnp.concatenate([w2[l0, d], zpad], axis=0).astype(BF16) for d in range(2)],
        "a0": a0[l0],
        "a2p": jnp.concatenate([zpad, a2[l0]], axis=0).astype(BF16),
        "k_k": k_k[l0], "k_a": k_a[l0], "r_k": r_k[l0], "ln_w": ln_w[l0], "ln_b": ln_b[l0],
        "final_g": final_g,
    }
    nb = c.shape[0]
    cond8 = jnp.concatenate([c_ctx[None, :], c, jnp.zeros((8 - 1 - nb, D_MODEL), F32)], axis=0)
    mod3 = _modulation(cond8, w_mod[l0], b_mod[l0]).reshape(8, 1, 3 * D_MODEL)

    y_prompt, (sf, sb, wf, wb) = _group(x_prompt, mod3, 0, False, None, wts)
    lat_states = (state_ssm_fwd[:, l0], state_ssm_bwd[:, l0], state_wkv_fwd[:, l0], state_wkv_bwd[:, l0])
    y_sample, _ = _group(x_sample, mod3, 1, True, lat_states, wts)
    return (y_prompt, y_sample, sf[:, None], sb[:, None], wf[:, None], wb[:, None])
```

```python
import functools
import math

import jax
import jax.numpy as jnp
from jax import lax
from jax.experimental import pallas as pl
from jax.experimental.pallas import tpu as pltpu

F32 = jnp.float32
BF16 = jnp.bfloat16
HI = lax.Precision.HIGHEST

D_MODEL = 1024
GRID_W = 64
NORM_EPS = 1e-6
SSM_D_INNER = 2048
SSM_HEADDIM = 64
SSM_HEADS = 32
SSM_GROUPS = 4
SSM_STATE = 128
SSM_CHUNK = 128
WKV_WIDTH = 1024
WKV_HEADSIZE = 64
WKV_HEADS = 16
WKV_RANK = 64
WKV_GN_EPS = 64e-5
WKV_CHUNK = 64
XBC_W = SSM_D_INNER + 2 * SSM_GROUPS * SSM_STATE
RWKV_SCAN_W = 3 * WKV_WIDTH + 2 * WKV_RANK
ZA_END = SSM_D_INNER
XBC_END = ZA_END + XBC_W
DT_END = XBC_END + SSM_HEADS
RW_END = DT_END + RWKV_SCAN_W
ZB_END = RW_END + WKV_WIDTH
IN_W = ZB_END + 2 * D_MODEL

LANES = 128
N_PAIRS = WKV_HEADS // 2
SSM_PAIRS = SSM_HEADS // 2
WKV_SUB = 4
SSD_SUB = 4
FINAL_TM = 512
ADALN_TM = 1024
CM_COLS = 16
HALO = 16
GL_OFF = SSM_D_INNER
ZB_OFF = GL_OFF + 2 * D_MODEL
DT_OFF = ZB_OFF + WKV_WIDTH
ACT_COLS = 2 * LANES
PLAIN_W = 21 * ACT_COLS
VMEM_LIMIT = 48 * 1024 * 1024
MERGE_VMEM_LIMIT = 56 * 1024 * 1024


def _cparams(sem, vmem_limit=VMEM_LIMIT):
    return pltpu.CompilerParams(dimension_semantics=sem, vmem_limit_bytes=vmem_limit)


def _sigmoid(x):
    return 1.0 / (1.0 + jnp.exp2(x * (-1.0 / math.log(2.0))))


def _silu(x):
    return x * _sigmoid(x)


def _softplus(x):
    return jnp.maximum(x, 0.0) + jnp.log(1.0 + jnp.exp(-jnp.abs(x)))


def _dot(a, b, precision=None):
    return jnp.dot(a, b, preferred_element_type=F32, precision=precision)


def _dot_nt(a, b, precision=None):
    return lax.dot_general(a, b, (((1,), (1,)), ((), ())), preferred_element_type=F32, precision=precision)


def _bdot(a, b):
    return _dot(a.astype(BF16), b.astype(BF16))


def _bdot_nt(a, b):
    return _dot_nt(a.astype(BF16), b.astype(BF16))


def _mod_kernel(c_ref, w_ref, b_ref, o_ref):
    c = c_ref[...]
    o_ref[...] = _dot(_silu(c), w_ref[...], HI) + b_ref[...]


def _modulation(cond8, w_mod, b_mod):
    n = w_mod.shape[1]
    tn = 1024
    return pl.pallas_call(
        _mod_kernel,
        grid=(n // tn,),
        in_specs=[pl.BlockSpec((8, D_MODEL), lambda j: (0, 0)),
                  pl.BlockSpec((D_MODEL, tn), lambda j: (0, j)),
                  pl.BlockSpec((1, tn), lambda j: (0, j))],
        out_specs=pl.BlockSpec((8, tn), lambda j: (0, j)),
        out_shape=jax.ShapeDtypeStruct((8, n), F32),
        compiler_params=_cparams(("arbitrary",)),
        name="modulation",
    )(cond8, w_mod, b_mod.reshape(1, n))


def _h_kernel(x_ref, mod_ref, g_ref, h_ref, *, grid_rows):
    if grid_rows is None:
        x = x_ref[...]
    else:
        x = x_ref[0].reshape(grid_rows * CM_COLS, D_MODEL)
    y = x * lax.rsqrt(jnp.mean(x * x, axis=-1, keepdims=True) + NORM_EPS) * g_ref[...]
    m = mod_ref[0]
    shift = m[:, :D_MODEL]
    scale = m[:, D_MODEL:2 * D_MODEL]
    h = (y * (1.0 + scale) + shift).astype(BF16)
    if grid_rows is not None:
        n = grid_rows * CM_COLS
        dst = lax.broadcasted_iota(jnp.int32, (n, n), 0)
        src = lax.broadcasted_iota(jnp.int32, (n, n), 1)
        perm = (src == (dst % grid_rows) * CM_COLS + dst // grid_rows).astype(BF16)
        h = _dot(perm, h).astype(BF16)
    h_ref[...] = h


def _adaln_col_major(x, mod3, norm_g, row0):
    b, l, _ = x.shape
    rows = l // GRID_W
    tile = rows * CM_COLS
    return pl.pallas_call(
        functools.partial(_h_kernel, grid_rows=rows),
        grid=(b, GRID_W // CM_COLS),
        in_specs=[pl.BlockSpec((1, rows, CM_COLS, D_MODEL), lambda i, j: (i, 0, j, 0)),
                  pl.BlockSpec((1, 1, 3 * D_MODEL), lambda i, j: (row0 + i, 0, 0)),
                  pl.BlockSpec((1, D_MODEL), lambda i, j: (0, 0))],
        out_specs=pl.BlockSpec((tile, D_MODEL), lambda i, j: (i * (GRID_W // CM_COLS) + j, 0)),
        out_shape=jax.ShapeDtypeStruct((b * l, D_MODEL), BF16),
        compiler_params=_cparams(("arbitrary", "arbitrary")),
        name="adaln_norm_cm",
    )(x.reshape(b, rows, GRID_W, D_MODEL), mod3, norm_g.reshape(1, D_MODEL))


def _adaln(x2, mod3, norm_g, row0, tiles_per_row, tm):
    t = x2.shape[0]
    return pl.pallas_call(
        functools.partial(_h_kernel, grid_rows=None),
        grid=(t // tm,),
        in_specs=[pl.BlockSpec((tm, D_MODEL), lambda i: (i, 0)),
                  pl.BlockSpec((1, 1, 3 * D_MODEL), lambda i: (row0 + i // tiles_per_row, 0, 0)),
                  pl.BlockSpec((1, D_MODEL), lambda i: (0, 0))],
        out_specs=pl.BlockSpec((tm, D_MODEL), lambda i: (i, 0)),
        out_shape=jax.ShapeDtypeStruct((t, D_MODEL), BF16),
        compiler_params=_cparams(("arbitrary",)),
        name="adaln_norm",
    )(x2, mod3, norm_g.reshape(1, D_MODEL))


def _proj_kernel(*refs, mode, tm, seq_len):
    if mode == "plain":
        h_ref, w_ref, o_ref = refs
        o_ref[...] = _dot(h_ref[...], w_ref[...])
        return
    halo = tm != seq_len
    cw, rb_rows, row_chunk = EPI_BLOCK[mode]
    rc_rows = min(row_chunk, tm)
    if halo:
        h_ref, hp_ref, hn_ref, w_ref = refs[:4]
        n_cb = w_ref.shape[1] // cw
        params = refs[4:len(refs) - 1 - n_cb]
        start = pl.program_id(1) * tm
        keep_prev = jnp.where((start & (seq_len - 1)) != 0, 1.0, 0.0)
        keep_next = jnp.where(((start + tm) & (seq_len - 1)) != 0, 1.0, 0.0)
    else:
        h_ref, w_ref = refs[:2]
        n_cb = w_ref.shape[1] // cw
        params = refs[2:len(refs) - 1 - n_cb]
    o_ref = refs[len(refs) - 1 - n_cb]
    p_scrs = refs[len(refs) - n_cb:]
    pad = 8
    sub = lax.broadcasted_iota(jnp.int32, (pad, cw), 0)
    nb = rb_rows // pad
    sub3 = lax.broadcasted_iota(jnp.int32, (nb, pad, cw), 1)
    n_rc = tm // rc_rows

    def matmul_unit(cb, rc):
        cols = slice(cb * cw, (cb + 1) * cw)
        p_scr = p_scrs[cb]
        rows = slice(rc * rc_rows, (rc + 1) * rc_rows)
        if rc == 0 and halo:
            lhs = jnp.concatenate([h_ref[rows, :], hp_ref[...], hn_ref[...]], axis=0)
        else:
            lhs = h_ref[rows, :]
        p = _dot(lhs, w_ref[:, cols])
        p_scr[pad + rc * rc_rows:pad + (rc + 1) * rc_rows, :] = p[:rc_rows]
        if rc == 0 and halo:
            prow = p[rc_rows + HALO - 1:rc_rows + HALO] * keep_prev
            nrow = p[rc_rows + HALO:rc_rows + HALO + 1] * keep_next
            p_scr[0:pad, :] = jnp.where(sub == pad - 1, prow, 0.0)
            p_scr[pad + tm:2 * pad + tm, :] = jnp.where(sub == 0, nrow, 0.0)
        elif rc == 0:
            p_scr[0:pad, :] = jnp.zeros((pad, cw), F32)
            p_scr[pad + tm:2 * pad + tm, :] = jnp.zeros((pad, cw), F32)

    def epilogue_unit(cb, rc):
        cols = slice(cb * cw, (cb + 1) * cw)
        per = rc_rows // rb_rows
        for rb in range(rc * per, (rc + 1) * per):
            r0 = pad + rb * rb_rows
            rs = slice(rb * rb_rows, (rb + 1) * rb_rows)
            win = p_scrs[cb][r0 - pad:r0 + rb_rows + pad, :].reshape(nb + 2, pad, cw)
            cur = win[1:nb + 1]
            down = pltpu.roll(win[:nb + 1], 1, 1)
            up = pltpu.roll(win[1:], pad - 1, 1)
            prev = jnp.where(sub3 == 0, down[:nb], down[1:])
            nxt = jnp.where(sub3 == pad - 1, up[1:], up[:nb])
            if mode == "conv":
                cw_ref, cb_ref = params
                out = _silu(cw_ref[0:1, cols] * prev + cw_ref[1:2, cols] * cur
                            + cw_ref[2:3, cols] * nxt + cb_ref[:, cols])
            else:
                (mu_ref,) = params
                out = cur + mu_ref[:, cols] * (0.5 * (prev + nxt) - cur)
            o_ref[rs, cols] = out.reshape(rb_rows, cw)

    units = [(cb, rc) for cb in range(n_cb) for rc in range(n_rc)]
    matmul_unit(*units[0])
    for i, unit in enumerate(units):
        if i + 1 < len(units):
            matmul_unit(*units[i + 1])
        epilogue_unit(*unit)


EPI_BLOCK = {"conv": (256, 32, 128), "shift": (640, 16, 512)}


def _gates_kernel(h_ref, w_ref, g_ref, dt_ref):
    h = h_ref[...]
    for c0 in range(0, DT_OFF, ACT_COLS):
        p = _dot(h, w_ref[:, c0:c0 + ACT_COLS])
        p = _sigmoid(p) if GL_OFF <= c0 < ZB_OFF else _silu(p)
        g_ref[:, c0:c0 + ACT_COLS] = p.astype(BF16)
    dt_ref[...] = _dot(h, w_ref[:, DT_OFF:])


def _project_gates(h, w, tm=512):
    t, k = h.shape
    return pl.pallas_call(
        _gates_kernel,
        grid=(t // tm,),
        in_specs=[pl.BlockSpec((tm, k), lambda i: (i, 0)),
                  pl.BlockSpec((k, PLAIN_W), lambda i: (0, 0), pipeline_mode=pl.Buffered(1))],
        out_specs=[pl.BlockSpec((tm, DT_OFF), lambda i: (i, 0)),
                   pl.BlockSpec((tm, PLAIN_W - DT_OFF), lambda i: (i, 0))],
        out_shape=[jax.ShapeDtypeStruct((t, DT_OFF), BF16),
                   jax.ShapeDtypeStruct((t, PLAIN_W - DT_OFF), F32)],
        compiler_params=_cparams(("arbitrary",)),
        name="in_proj_gates",
    )(h, w)


def _project(h, w, tn, mode="plain", seq_len=None, params=(), tm=512):
    t, k = h.shape
    n = w.shape[1]
    scratch = []
    neighbours = mode in EPI_BLOCK
    if neighbours:
        tm = min(tm, seq_len)
        assert seq_len & (seq_len - 1) == 0 and seq_len % tm == 0 and tm % HALO == 0
        cw = EPI_BLOCK[mode][0]
        scratch = [pltpu.VMEM((tm + 16, cw), F32)] * (tn // cw)
    in_specs = [pl.BlockSpec((tm, k), lambda j, i: (i, 0))]
    args = [h]
    if neighbours and tm != seq_len:
        per = tm // HALO
        in_specs += [pl.BlockSpec((HALO, k), lambda j, i: (jnp.maximum(i * per - 1, 0), 0)),
                     pl.BlockSpec((HALO, k), lambda j, i: (jnp.minimum((i + 1) * per, t // HALO - 1), 0))]
        args += [h, h]
    w_mode = pl.Buffered(1) if tn == n else None
    in_specs.append(pl.BlockSpec((k, tn), lambda j, i: (0, j), pipeline_mode=w_mode))
    args.append(w)
    for prm in params:
        in_specs.append(pl.BlockSpec((prm.shape[0], tn), lambda j, i: (0, j)))
        args.append(prm)
    return pl.pallas_call(
        functools.partial(_proj_kernel, mode=mode, tm=tm, seq_len=seq_len),
        grid=(n // tn, t // tm),
        in_specs=in_specs,
        out_specs=pl.BlockSpec((tm, tn), lambda j, i: (i, j)),
        out_shape=jax.ShapeDtypeStruct((t, n), F32),
        scratch_shapes=scratch,
        compiler_params=_cparams(("arbitrary", "arbitrary")),
        name="in_proj_" + mode,
    )(*args)


def _ssd_kernel(*refs, rev, has_init, has_prev, n_sub):
    refs = list(refs)
    xs_ref, b_ref, c_ref, cs_ref, cst_ref, dtt_ref, ddt_ref = refs[:7]
    pos = 7
    s0_ref = None
    if has_init:
        s0_ref = refs[pos]
        pos += 1
    yprev_ref = dskip_ref = None
    if has_prev:
        yprev_ref, dskip_ref = refs[pos], refs[pos + 1]
        pos += 2
    y_ref, sfin_ref, s_ref = refs[pos], refs[pos + 1], refs[pos + 2]

    q = SSM_CHUNK
    c = pl.program_id(1)

    @pl.when(c == 0)
    def _():
        if has_init:
            s_ref[...] = s0_ref[0]
        else:
            s_ref[...] = jnp.zeros_like(s_ref)

    ii = lax.broadcasted_iota(jnp.int32, (q, q), 0)
    jj = lax.broadcasted_iota(jnp.int32, (q, q), 1)
    incl = (jj >= ii) if rev else (jj <= ii)
    lane_lo = jj < SSM_HEADDIM
    row_lo = ii < SSM_HEADDIM
    last = 0 if rev else q - 1

    m_lo = lane_lo.astype(BF16)
    m_hi = jnp.logical_not(lane_lo).astype(BF16)
    subs = list(range(n_sub))[::-1] if rev else list(range(n_sub))
    for j, g in [(j, g) for j in subs for g in range(SSM_GROUPS)]:
        rows = slice(j * q, (j + 1) * q)
        cs = cs_ref[0, 0, rows, :]
        cs_t = cst_ref[0, 0, j]
        dt_t = dtt_ref[0, 0, j]
        dd_t = ddt_ref[0, 0, j]
        bm = b_ref[0, rows, g * SSM_STATE:(g + 1) * SSM_STATE].astype(BF16)
        cm = c_ref[0, rows, g * SSM_STATE:(g + 1) * SSM_STATE].astype(BF16)
        cb = _dot_nt(cm, bm)
        ps = [g * (SSM_PAIRS // SSM_GROUPS) + jp for jp in range(SSM_PAIRS // SSM_GROUPS)]
        idx = range(len(ps))
        lanes = [slice(p * LANES, (p + 1) * LANES) for p in ps]
        xs = [xs_ref[0, rows, lanes[i]] for i in idx]
        cs_col = {h: cs[:, h:h + 1] for p in ps for h in (2 * p, 2 * p + 1)}
        cs_pair = [jnp.where(lane_lo, cs_col[2 * p], cs_col[2 * p + 1]) for p in ps]
        w_pair = []
        for p in ps:
            w_heads = []
            for h in (2 * p, 2 * p + 1):
                seg = cs_col[h] - cs_t[h:h + 1, :]
                lm = jnp.where(incl, jnp.exp(seg), 0.0)
                w_heads.append((cb * lm * dt_t[h:h + 1, :]).astype(BF16))
            w_pair.append(jnp.concatenate(w_heads, axis=1))
        xs_b = [q_.astype(BF16) for q_ in xs]
        y_diag = [_dot(w_pair[i], jnp.concatenate([xs_b[i] * m_lo, xs_b[i] * m_hi], axis=0)) for i in idx]
        s_pair = [s_ref[p] for p in ps]
        y_off = [_dot_nt(cm, s_pair[i].astype(BF16)) * jnp.exp(cs_pair[i]) for i in idx]
        for i in idx:
            y = y_diag[i] + y_off[i]
            if has_prev:
                y = y + yprev_ref[0, rows, lanes[i]] + dskip_ref[:, lanes[i]] * xs[i]
            y_ref[0, rows, lanes[i]] = y
        scale = [jnp.where(row_lo, dd_t[2 * p:2 * p + 1, :], dd_t[2 * p + 1:2 * p + 2, :]) for p in ps]
        upd = [_dot((xs[i].T * scale[i]).astype(BF16), bm) for i in idx]
        for i, p in enumerate(ps):
            end_col = jnp.where(row_lo, cs_t[2 * p:2 * p + 1, last:last + 1], cs_t[2 * p + 1:2 * p + 2, last:last + 1])
            s_ref[p] = s_pair[i] * jnp.exp(end_col) + upd[i]

    @pl.when(c == pl.num_programs(1) - 1)
    def _():
        sfin_ref[0] = s_ref[...]


def _ssd_prep_kernel(raw_ref, alog_ref, dtb_ref, cs_ref, cst_ref, dtt_ref, ddt_ref, *, nchunks):
    q = SSM_CHUNK
    ii = lax.broadcasted_iota(jnp.int32, (q, q), 0)
    jj = lax.broadcasted_iota(jnp.int32, (q, q), 1)
    tri = [[(jj <= ii).astype(BF16)], [(jj >= ii).astype(BF16)]]
    last = [q - 1, 0]
    chains = [(d, ck) for d in range(2) for ck in range(nchunks)]
    rows = [slice(ck * q, (ck + 1) * q) for _, ck in chains]
    dt = [_softplus(raw_ref[0, rows[i], :] + dtb_ref[d:d + 1, :]) for i, (d, _) in enumerate(chains)]
    a = [dt[i] * (-jnp.exp(alog_ref[d:d + 1, :])) for i, (d, _) in enumerate(chains)]
    cs = [_mm(tri[d], _pieces(a[i], 3)) for i, (d, _) in enumerate(chains)]
    dd = [dt[i] * jnp.exp(cs[i][last[d]:last[d] + 1, :] - cs[i]) for i, (d, _) in enumerate(chains)]
    for i, (d, ck) in enumerate(chains):
        cs_ref[d, 0, rows[i], :] = cs[i]
        cst_ref[d, 0, ck] = cs[i].T
        dtt_ref[d, 0, ck] = dt[i].T
        ddt_ref[d, 0, ck] = dd[i].T


def _ssd_prep(p_dt3, alog2, dtb2):
    b, l, _ = p_dt3.shape
    rows = min(l, 8 * SSM_CHUNK)
    nck = rows // SSM_CHUNK
    dt_blk = 0
    t_spec = pl.BlockSpec((2, 1, nck, SSM_CHUNK, LANES), lambda i, j: (0, i, j, 0, 0))
    t_shape = jax.ShapeDtypeStruct((2, b, l // SSM_CHUNK, SSM_CHUNK, LANES), F32)
    return pl.pallas_call(
        functools.partial(_ssd_prep_kernel, nchunks=nck),
        grid=(b, l // rows),
        in_specs=[pl.BlockSpec((1, rows, LANES), lambda i, j: (i, j, dt_blk)),
                  pl.BlockSpec((2, LANES), lambda i, j: (0, 0)),
                  pl.BlockSpec((2, LANES), lambda i, j: (0, 0))],
        out_specs=[pl.BlockSpec((2, 1, rows, LANES), lambda i, j: (0, i, j, 0)), t_spec, t_spec, t_spec],
        out_shape=[jax.ShapeDtypeStruct((2, b, l, LANES), F32), t_shape, t_shape, t_shape],
        compiler_params=_cparams(("arbitrary", "arbitrary")),
        name="ssd_prep",
    )(p_dt3, alog2, dtb2)


def _ssd_scan(xbc, prep, s0, yprev, dskip, rev):
    b, l, _ = xbc.shape
    n_sub = min(SSD_SUB, l // SSM_CHUNK)
    q = SSM_CHUNK * n_sub
    nc = l // q
    cidx = (lambda c: nc - 1 - c) if rev else (lambda c: c)
    d = 1 if rev else 0
    t_spec = pl.BlockSpec((1, 1, n_sub, SSM_CHUNK, LANES), lambda i, c: (d, i, cidx(c), 0, 0))
    in_specs = [pl.BlockSpec((1, q, SSM_D_INNER), lambda i, c: (i, cidx(c), 0)),
                pl.BlockSpec((1, q, 512), lambda i, c: (i, cidx(c), SSM_D_INNER // 512)),
                pl.BlockSpec((1, q, 512), lambda i, c: (i, cidx(c), SSM_D_INNER // 512 + 1)),
                pl.BlockSpec((1, 1, q, LANES), lambda i, c: (d, i, cidx(c), 0)),
                t_spec, t_spec, t_spec]
    args = [xbc, xbc, xbc, *prep]
    if s0 is not None:
        in_specs.append(pl.BlockSpec((1, SSM_PAIRS, LANES, SSM_STATE), lambda i, c: (i, 0, 0, 0)))
        args.append(s0)
    if yprev is not None:
        in_specs.append(pl.BlockSpec((1, q, SSM_D_INNER), lambda i, c: (i, cidx(c), 0)))
        in_specs.append(pl.BlockSpec((1, SSM_D_INNER), lambda i, c: (0, 0)))
        args += [yprev, dskip]
    kern = functools.partial(_ssd_kernel, rev=rev, has_init=s0 is not None, has_prev=yprev is not None, n_sub=n_sub)
    return pl.pallas_call(
        kern,
        grid=(b, nc),
        in_specs=in_specs,
        out_specs=[pl.BlockSpec((1, q, SSM_D_INNER), lambda i, c: (i, cidx(c), 0)),
                   pl.BlockSpec((1, SSM_PAIRS, LANES, SSM_STATE), lambda i, c: (i, 0, 0, 0))],
        out_shape=[jax.ShapeDtypeStruct((b, l, SSM_D_INNER), F32),
                   jax.ShapeDtypeStruct((b, SSM_PAIRS, LANES, SSM_STATE), F32)],
        scratch_shapes=[pltpu.VMEM((SSM_PAIRS, LANES, SSM_STATE), F32)],
        compiler_params=_cparams(("arbitrary", "arbitrary")),
        name="ssd_bwd" if rev else "ssd_fwd",
    )(*args)


CUM_PIECES = 2


def _pieces(x, n):
    out = []
    for i in range(n):
        h = x.astype(BF16)
        out.append(h)
        if i + 1 < n:
            x = x - h.astype(F32)
    return out


def _mm(a_pieces, b_pieces, nt=False):
    dot = _dot_nt if nt else _dot
    depth = max(len(a_pieces), len(b_pieces))
    acc = None
    for i in reversed(range(len(a_pieces))):
        for j in reversed(range(len(b_pieces))):
            if i + j < depth:
                t = dot(a_pieces[i], b_pieces[j])
                acc = t if acc is None else acc + t
    return acc


def _block_diag(y, m_lo, m_hi):
    return jnp.concatenate([y * m_lo, y * m_hi], axis=0)


def _wkv_kernel(*refs, rev, has_init, final, grid_rows):
    refs = list(refs)
    (rw_ref, w0_ref, w2_ref, a0_ref, a2_ref, kk_ref, ka_ref) = refs[:7]
    pos = 7
    s0_ref = None
    if has_init:
        s0_ref = refs[pos]
        pos += 1
    of_ref = rk_ref = lnw_ref = lnb_ref = None
    if final:
        of_ref, rk_ref, lnw_ref, lnb_ref = refs[pos:pos + 4]
        pos += 4
    o_ref, sfin_ref, s_ref = refs[pos], refs[pos + 1], refs[pos + 2]

    n = WKV_CHUNK
    c = pl.program_id(1)

    @pl.when(c == 0)
    def _():
        if has_init:
            s_ref[...] = s0_ref[0]
        else:
            s_ref[...] = jnp.zeros_like(s_ref)

    xl = rw_ref[0, :, 3 * WKV_WIDTH:3 * WKV_WIDTH + LANES]
    a_full = _sigmoid(a0_ref[...] + _dot(xl.astype(BF16), a2_ref[...]))
    xw = w0_ref[...] + _dot(jnp.tanh(xl).astype(BF16), w2_ref[...])
    lw_full = -math.exp(-0.5) * _sigmoid(xw)

    ti = lax.broadcasted_iota(jnp.int32, (n, LANES), 0)
    li = lax.broadcasted_iota(jnp.int32, (n, LANES), 1)
    si = li & (WKV_HEADSIZE - 1)
    strict = (si > ti) if rev else (si < ti)
    incl = (si >= ti) if rev else (si <= ti)
    eye = (si == ti).astype(BF16)
    m_lo = (li < WKV_HEADSIZE).astype(BF16)
    m_hi = (li >= WKV_HEADSIZE).astype(BF16)
    tq = lax.broadcasted_iota(jnp.int32, (n, n), 0)
    sq = lax.broadcasted_iota(jnp.int32, (n, n), 1)
    tri = jnp.concatenate([((sq >= tq) if rev else (sq <= tq)).astype(BF16)] * CUM_PIECES, axis=1)
    r2 = lax.broadcasted_iota(jnp.int32, (LANES, LANES), 0)
    c2 = lax.broadcasted_iota(jnp.int32, (LANES, LANES), 1)
    same_head = (r2 < WKV_HEADSIZE) == (c2 < WKV_HEADSIZE)
    ones_bd = same_head.astype(BF16)
    last = 0 if rev else n - 1

    def level_mask(m):
        same = (ti >> (m.bit_length())) == (si >> (m.bit_length()))
        t_hi = (ti & m) != 0
        s_hi = (si & m) != 0
        if rev:
            return same & jnp.logical_not(t_hi) & s_hi
        return same & t_hi & jnp.logical_not(s_hi)

    levels = []
    m = 2
    while m < n:
        levels.append(level_mask(m).astype(BF16))
        m *= 2
    level1 = level_mask(1).astype(BF16)

    def bd(y):
        return _block_diag(y.astype(BF16), m_lo, m_hi)

    def seg_sum(x):
        return _dot(x.astype(BF16), ones_bd)

    def lane_blk(p, base=0):
        return slice(base + p * LANES, base + (p + 1) * LANES)

    def rows(j):
        return slice(j * n, (j + 1) * n)

    subs = list(range(WKV_SUB))[::-1] if rev else list(range(WKV_SUB))
    chains = [(j, p) for j in subs for p in range(N_PAIRS)]
    idx = range(len(chains))
    r = [rw_ref[0, rows(j), lane_blk(p)] for j, p in chains]
    k = [rw_ref[0, rows(j), lane_blk(p, WKV_WIDTH)] for j, p in chains]
    v = [rw_ref[0, rows(j), lane_blk(p, 2 * WKV_WIDTH)] for j, p in chains]
    a = [a_full[rows(j), lane_blk(p)] for j, p in chains]
    lw = [lw_full[rows(j), lane_blk(p)] for j, p in chains]
    kkr = [k[i] * kk_ref[p] for i, (j, p) in enumerate(chains)]
    ss = [seg_sum(q * q) for q in kkr]
    kk = [kkr[i] / jnp.maximum(jnp.sqrt(ss[i]), 1e-12) for i in idx]
    kmod = [k[i] * (1.0 + (a[i] - 1.0) * ka_ref[p]) for i, (j, p) in enumerate(chains)]
    kka = [kk[i] * a[i] for i in idx]

    lg = [_dot(tri, jnp.concatenate(_pieces(q, CUM_PIECES), axis=0)) for q in lw]
    g_inv = [jnp.exp(-q) for q in lg]
    rh = [r[i] * jnp.exp(lg[i]) for i in idx]
    kh = [kmod[i] * g_inv[i] for i in idx]
    ah = [kka[i] * g_inv[i] for i in idx]
    bh = [-kk[i] * jnp.exp(lg[i] - lw[i]) for i in idx]
    g_end = [jnp.exp(q[last:last + 1, :]) for q in lg]

    lhs = [jnp.concatenate([bh[i], rh[i]], axis=0).astype(BF16) for i in idx]
    rhs = [jnp.concatenate([bd(ah[i]), bd(kh[i])], axis=0) for i in idx]
    aak = [_dot_nt(lhs[i], rhs[i]) for i in idx]
    a_ab = [jnp.where(strict, q[:n, :LANES], 0.0).astype(BF16) for q in aak]
    a_ra = [jnp.where(incl, q[n:, :LANES], 0.0).astype(BF16) for q in aak]
    a_bkrk = [jnp.concatenate([jnp.where(strict, q[:n, LANES:], 0.0), jnp.where(incl, q[n:, LANES:], 0.0)],
                              axis=0).astype(BF16) for q in aak]

    x = [eye + q * level1 for q in a_ab]
    for lm in levels:
        pm = [_dot(x[i], bd(a_ab[i] * lm)).astype(BF16) for i in idx]
        x = [x[i] + _dot(pm[i], bd(x[i])).astype(BF16) for i in idx]
    gv = [_dot(a_bkrk[i], bd(v[i])) for i in idx]
    ake = [(jnp.concatenate([ah[i], kh[i]], axis=0) * g_end[i]).astype(BF16) for i in idx]

    o = [None] * len(chains)
    for jj in range(WKV_SUB):
        ids = list(range(jj * N_PAIRS, (jj + 1) * N_PAIRS))
        s_bd = {i: s_ref[chains[i][1]] for i in ids}
        x0r = {i: _dot_nt(lhs[i], s_bd[i].astype(BF16)) for i in ids}
        u = {i: _dot(x[i], bd(x0r[i][:n] + gv[i][:n])) for i in ids}
        for i in ids:
            o[i] = x0r[i][n:] + gv[i][n:] + _dot(a_ra[i], bd(u[i]))
        uvt = {i: jnp.concatenate([u[i], v[i]], axis=0).T.astype(BF16) for i in ids}
        upd = {i: _dot(uvt[i], ake[i]) for i in ids}
        for i in ids:
            s_ref[chains[i][1]] = s_bd[i] * g_end[i] + jnp.where(same_head, upd[i], 0.0)

    if final:
        o = [o[i] + of_ref[0, p, rows(j), :] for i, (j, p) in enumerate(chains)]
        mu = [seg_sum(q) * (1.0 / WKV_HEADSIZE) for q in o]
        d = [o[i] - mu[i] for i in idx]
        var = [seg_sum(q * q) * (1.0 / WKV_HEADSIZE) for q in d]
        bonus = [seg_sum(r[i] * kmod[i] * rk_ref[p]) * v[i] for i, (j, p) in enumerate(chains)]
        o = [d[i] * lax.rsqrt(var[i] + WKV_GN_EPS) * lnw_ref[p] + lnb_ref[p] + bonus[i]
             for i, (j, p) in enumerate(chains)]
    if grid_rows is None:
        for i, (j, p) in enumerate(chains):
            o_ref[0, p, rows(j), :] = o[i]
    else:
        step = (pl.num_programs(1) - 1 - c) if rev else c
        cols_per_sub = n // grid_rows
        for i, (j, p) in enumerate(chains):
            for wl in range(cols_per_sub):
                w = (step * WKV_SUB + j) * cols_per_sub + wl
                o_ref[0, p, pl.ds(w, grid_rows, stride=GRID_W), :] = o[i][wl * grid_rows:(wl + 1) * grid_rows]

    @pl.when(c == pl.num_programs(1) - 1)
    def _():
        sfin_ref[0] = s_ref[...]


def _pairs(vec):
    return vec.reshape(N_PAIRS, 1, LANES)


def _wkv_scan(rw_s, w0, w2p, a0, a2p, k_k, k_a, s0, o_f, r_k, ln_w, ln_b, rev, to_row_major=False):
    b, l, w = rw_s.shape
    n = WKV_CHUNK * WKV_SUB
    nc = l // n
    cidx = (lambda c: nc - 1 - c) if rev else (lambda c: c)
    final = o_f is not None
    vec_spec = pl.BlockSpec((N_PAIRS, 1, LANES), lambda i, c: (0, 0, 0))
    st_spec = pl.BlockSpec((1, N_PAIRS, LANES, LANES), lambda i, c: (i, 0, 0, 0))
    in_specs = [pl.BlockSpec((1, n, w), lambda i, c: (i, cidx(c), 0)),
                pl.BlockSpec((1, WKV_WIDTH), lambda i, c: (0, 0)),
                pl.BlockSpec((LANES, WKV_WIDTH), lambda i, c: (0, 0)),
                pl.BlockSpec((1, WKV_WIDTH), lambda i, c: (0, 0)),
                pl.BlockSpec((LANES, WKV_WIDTH), lambda i, c: (0, 0)),
                vec_spec, vec_spec]
    args = [rw_s, w0.reshape(1, WKV_WIDTH), w2p, a0.reshape(1, WKV_WIDTH), a2p, _pairs(k_k), _pairs(k_a)]
    if s0 is not None:
        in_specs.append(st_spec)
        args.append(s0)
    if final:
        in_specs += [pl.BlockSpec((1, N_PAIRS, n, LANES), lambda i, c: (i, 0, cidx(c), 0)),
                     vec_spec, vec_spec, vec_spec]
        args += [o_f, _pairs(r_k), _pairs(ln_w), _pairs(ln_b)]
    o_spec = pl.BlockSpec((1, N_PAIRS, n, LANES), lambda i, c: (i, 0, cidx(c), 0))
    grid_rows = None
    if to_row_major:
        o_spec = pl.BlockSpec((1, N_PAIRS, l, LANES), lambda i, c: (i, 0, 0, 0))
        grid_rows = l // GRID_W
        assert WKV_CHUNK % grid_rows == 0
    kern = functools.partial(_wkv_kernel, rev=rev, has_init=s0 is not None, final=final, grid_rows=grid_rows)
    return pl.pallas_call(
        kern,
        grid=(b, nc),
        in_specs=in_specs,
        out_specs=[o_spec, st_spec],
        out_shape=[jax.ShapeDtypeStruct((b, N_PAIRS, l, LANES), F32),
                   jax.ShapeDtypeStruct((b, N_PAIRS, LANES, LANES), F32)],
        scratch_shapes=[pltpu.VMEM((N_PAIRS, LANES, LANES), F32)],
        compiler_params=_cparams(("arbitrary", "arbitrary")),
        name="wkv_bwd" if rev else "wkv_fwd",
    )(*args)


def _final_kernel(y_ref, za_ref, o_ref, zb_ref, ga_ref, gb_ref, x_ref, mod_ref,
                  ng_ref, pa_ref, pb_ref, wo_ref, fg_ref, out_ref):
    y = y_ref[...] * za_ref[...].astype(F32)
    y = y * lax.rsqrt(jnp.mean(y * y, axis=-1, keepdims=True) + NORM_EPS) * ng_ref[...]
    u_a = _dot(y.astype(BF16), pa_ref[...])
    o = jnp.concatenate([jnp.concatenate([o_ref[s, p] for p in range(N_PAIRS)], axis=1)
                         for s in range(o_ref.shape[0])], axis=0)
    u_b = _dot((o * zb_ref[...].astype(F32)).astype(BF16), pb_ref[...])
    m = ga_ref[...].astype(F32) * u_a + gb_ref[...].astype(F32) * u_b
    out = _dot(m.astype(BF16), wo_ref[...])
    gate = mod_ref[0][:, 2 * D_MODEL:]
    xo = x_ref[...] + gate * out
    out_ref[...] = xo * lax.rsqrt(jnp.mean(xo * xo, axis=-1, keepdims=True) + NORM_EPS) * fg_ref[...]


def _final(y2, p_plain, o4, x2, mod3, ssm_norm_g, p_a, p_b, w_out, final_g, row0, tiles_per_row, tm):
    t = x2.shape[0]
    w1 = D_MODEL
    l = o4.shape[2]
    o_rows = min(tm, l)
    tiles_per_seq = l // o_rows

    def resident(shape):
        return pl.BlockSpec(shape, lambda i: (0, 0), pipeline_mode=pl.Buffered(1))

    return pl.pallas_call(
        _final_kernel,
        grid=(t // tm,),
        in_specs=[pl.BlockSpec((tm, SSM_D_INNER), lambda i: (i, 0)),
                  pl.BlockSpec((tm, SSM_D_INNER), lambda i: (i, 0)),
                  pl.BlockSpec((tm // o_rows, N_PAIRS, o_rows, LANES),
                               lambda i: (i // tiles_per_seq, 0, i % tiles_per_seq, 0)),
                  pl.BlockSpec((tm, w1), lambda i: (i, ZB_OFF // w1)),
                  pl.BlockSpec((tm, w1), lambda i: (i, GL_OFF // w1)),
                  pl.BlockSpec((tm, w1), lambda i: (i, GL_OFF // w1 + 1)),
                  pl.BlockSpec((tm, w1), lambda i: (i, 0)),
                  pl.BlockSpec((1, 1, 3 * D_MODEL), lambda i: (row0 + i // tiles_per_row, 0, 0)),
                  resident((1, SSM_D_INNER)),
                  resident((SSM_D_INNER, D_MODEL)),
                  resident((WKV_WIDTH, D_MODEL)),
                  resident((D_MODEL, D_MODEL)),
                  resident((1, D_MODEL))],
        out_specs=pl.BlockSpec((tm, D_MODEL), lambda i: (i, 0)),
        out_shape=jax.ShapeDtypeStruct((t, D_MODEL), F32),
        compiler_params=_cparams(("arbitrary",), MERGE_VMEM_LIMIT),
        name="merge_out",
    )(y2, p_plain, o4, p_plain, p_plain, p_plain, x2, mod3,
      ssm_norm_g.reshape(1, SSM_D_INNER), p_a, p_b, w_out, final_g.reshape(1, D_MODEL))


def _wkv_state_to_pairs(s):
    b = s.shape[0]
    s = s.reshape(b, N_PAIRS, 2, WKV_HEADSIZE, WKV_HEADSIZE)
    z = jnp.zeros_like(s[:, :, 0])
    top = jnp.concatenate([s[:, :, 0], z], axis=-1)
    bot = jnp.concatenate([z, s[:, :, 1]], axis=-1)
    return jnp.concatenate([top, bot], axis=-2)


def _wkv_state_from_pairs(s):
    b = s.shape[0]
    h = WKV_HEADSIZE
    return jnp.stack([s[:, :, :h, :h], s[:, :, h:, h:]], axis=2).reshape(b, WKV_HEADS, h, h)


def _group(x, mod3, row0, grid, states, wts):
    b, l, _ = x.shape
    t = b * l
    x2 = x.reshape(t, D_MODEL)
    h = _adaln(x2, mod3, wts["norm_g"], row0, (l if grid else t) // ADALN_TM, ADALN_TM)
    p_plain, p_dt = _project_gates(h, wts["w_plain"])
    p_dt3 = p_dt.reshape(b, l, PLAIN_W - DT_OFF)
    h_rw = _adaln_col_major(x, mod3, wts["norm_g"], row0) if grid else h

    xbc = _project(h, wts["w_xbc"], tn=XBC_W, mode="conv", seq_len=l,
                   params=(wts["conv_w"], wts["conv_b"].reshape(1, XBC_W))).reshape(b, l, XBC_W)
    s_f = s_b = None
    if states is not None:
        s_f = states[0].reshape(b, SSM_PAIRS, LANES, SSM_STATE)
        s_b = states[1].reshape(b, SSM_PAIRS, LANES, SSM_STATE)
    prep = _ssd_prep(p_dt3, wts["alog"], wts["dtb"])
    y_f, fs_f = _ssd_scan(xbc, prep, s_f, None, None, rev=False)
    y, fs_b = _ssd_scan(xbc, prep, s_b, y_f, wts["dskip"], rev=True)

    rw_s = _project(h_rw, wts["w_rw"], tn=RWKV_SCAN_W, mode="shift", seq_len=l,
                    params=(wts["shift_mu"].reshape(1, RWKV_SCAN_W),)).reshape(b, l, RWKV_SCAN_W)
    w_f = w_b = None
    if states is not None:
        w_f = _wkv_state_to_pairs(states[2])
        w_b = _wkv_state_to_pairs(states[3])
    o_f, fw_f = _wkv_scan(rw_s, wts["w0"][0], wts["w2p"][0], wts["a0"], wts["a2p"], wts["k_k"], wts["k_a"],
                          w_f, None, None, None, None, rev=False)
    o, fw_b = _wkv_scan(rw_s, wts["w0"][1], wts["w2p"][1], wts["a0"], wts["a2p"], wts["k_k"], wts["k_a"],
                        w_b, o_f, wts["r_k"], wts["ln_w"], wts["ln_b"], rev=True, to_row_major=grid)

    out = _final(y.reshape(t, SSM_D_INNER), p_plain, o, x2, mod3, wts["ssm_norm_g"],
                 wts["p_a"], wts["p_b"], wts["w_out"], wts["final_g"], row0,
                 (l if grid else t) // FINAL_TM, FINAL_TM)
    finals = (fs_f.reshape(b, SSM_HEADS, SSM_HEADDIM, SSM_STATE), fs_b.reshape(b, SSM_HEADS, SSM_HEADDIM, SSM_STATE),
              _wkv_state_from_pairs(fw_f), _wkv_state_from_pairs(fw_b))
    return out.reshape(b, l, D_MODEL), finals


def kernel(x_prompt, x_sample, state_ssm_fwd, state_ssm_bwd, state_wkv_fwd, state_wkv_bwd, c, c_ctx, w_mod, b_mod, norm_g, w_in, conv_w, conv_b, a_log, dt_bias, d_skip, ssm_norm_g, p_a, shift_mu, w0, w2, a0, a2, k_k, k_a, r_k, ln_w, ln_b, p_b, w_out, final_g):
    depth = w_mod.shape[0]
    assert depth == 1, "single-layer stack only"
    l0 = 0
    w_in0 = w_in[l0].astype(BF16)
    zpad = jnp.zeros((WKV_RANK, WKV_WIDTH), F32)
    w_plain = jnp.concatenate([w_in0[:, :ZA_END], w_in0[:, ZB_END:], w_in0[:, RW_END:ZB_END], w_in0[:, XBC_END:DT_END],
                               jnp.zeros((D_MODEL, PLAIN_W - DT_OFF - SSM_HEADS), BF16)], axis=1)
    wts = {
        "norm_g": norm_g[l0],
        "w_plain": w_plain,
        "w_xbc": w_in0[:, ZA_END:XBC_END],
        "w_rw": w_in0[:, DT_END:RW_END],
        "conv_w": conv_w[l0], "conv_b": conv_b[l0],
        "alog": jnp.pad(a_log[l0], ((0, 0), (0, LANES - SSM_HEADS))),
        "dtb": jnp.pad(dt_bias[l0], ((0, 0), (0, LANES - SSM_HEADS))),
        "dskip": jnp.repeat(d_skip[l0], SSM_HEADDIM).reshape(1, SSM_D_INNER),
        "ssm_norm_g": ssm_norm_g[l0],
        "p_a": p_a[l0].astype(BF16), "p_b": p_b[l0].astype(BF16), "w_out": w_out[l0].astype(BF16),
        "shift_mu": shift_mu[l0],
        "w0": w0[l0],
        "w2p": [jnp.concatenate([w2[l0, d], zpad], axis=0).astype(BF16) for d in range(2)],
        "a0": a0[l0],
        "a2p": jnp.concatenate([zpad, a2[l0]], axis=0).astype(BF16),
        "k_k": k_k[l0], "k_a": k_a[l0], "r_k": r_k[l0], "ln_w": ln_w[l0], "ln_b": ln_b[l0],
        "final_g": final_g,
    }
    nb = c.shape[0]
    cond8 = jnp.concatenate([c_ctx[None, :], c, jnp.zeros((8 - 1 - nb, D_MODEL), F32)], axis=0)
    mod3 = _modulation(cond8, w_mod[l0], b_mod[l0]).reshape(8, 1, 3 * D_MODEL)

    y_prompt, (sf, sb, wf, wb) = _group(x_prompt, mod3, 0, False, None, wts)
    lat_states = (state_ssm_fwd[:, l0], state_ssm_bwd[:, l0], state_wkv_fwd[:, l0], state_wkv_bwd[:, l0])
    y_sample, _ = _group(x_sample, mod3, 1, True, lat_states, wts)
    return (y_prompt, y_sample, sf[:, None], sb[:, None], wf[:, None], wb[:, None])
```

```python
import functools
import math

import jax
import jax.numpy as jnp
from jax import lax
from jax.experimental import pallas as pl
from jax.experimental.pallas import tpu as pltpu

F32 = jnp.float32
BF16 = jnp.bfloat16
HI = lax.Precision.HIGHEST

D_MODEL = 1024
GRID_W = 64
NORM_EPS = 1e-6
SSM_D_INNER = 2048
SSM_HEADDIM = 64
SSM_HEADS = 32
SSM_GROUPS = 4
SSM_STATE = 128
SSM_CHUNK = 128
WKV_WIDTH = 1024
WKV_HEADSIZE = 64
WKV_HEADS = 16
WKV_RANK = 64
WKV_GN_EPS = 64e-5
WKV_CHUNK = 64
XBC_W = SSM_D_INNER + 2 * SSM_GROUPS * SSM_STATE
RWKV_SCAN_W = 3 * WKV_WIDTH + 2 * WKV_RANK
ZA_END = SSM_D_INNER
XBC_END = ZA_END + XBC_W
DT_END = XBC_END + SSM_HEADS
RW_END = DT_END + RWKV_SCAN_W
ZB_END = RW_END + WKV_WIDTH
IN_W = ZB_END + 2 * D_MODEL

LANES = 128
N_PAIRS = WKV_HEADS // 2
SSM_PAIRS = SSM_HEADS // 2
WKV_SUB = 4
SSD_SUB = 4
FINAL_TM = 512
ADALN_TM = 1024
CM_COLS = 16
HALO = 16
GL_OFF = SSM_D_INNER
ZB_OFF = GL_OFF + 2 * D_MODEL
DT_OFF = ZB_OFF + WKV_WIDTH
ACT_COLS = 2 * LANES
PLAIN_W = 21 * ACT_COLS
VMEM_LIMIT = 48 * 1024 * 1024
MERGE_VMEM_LIMIT = 56 * 1024 * 1024


def _cparams(sem, vmem_limit=VMEM_LIMIT):
    return pltpu.CompilerParams(dimension_semantics=sem, vmem_limit_bytes=vmem_limit)


def _sigmoid(x):
    return 1.0 / (1.0 + jnp.exp2(x * (-1.0 / math.log(2.0))))


def _silu(x):
    return x * _sigmoid(x)


def _softplus(x):
    return jnp.maximum(x, 0.0) + jnp.log(1.0 + jnp.exp(-jnp.abs(x)))


def _dot(a, b, precision=None):
    return jnp.dot(a, b, preferred_element_type=F32, precision=precision)


def _dot_nt(a, b, precision=None):
    return lax.dot_general(a, b, (((1,), (1,)), ((), ())), preferred_element_type=F32, precision=precision)


def _bdot(a, b):
    return _dot(a.astype(BF16), b.astype(BF16))


def _bdot_nt(a, b):
    return _dot_nt(a.astype(BF16), b.astype(BF16))


def _mod_kernel(c_ref, w_ref, b_ref, o_ref):
    c = c_ref[...]
    o_ref[...] = _dot(_silu(c), w_ref[...], HI) + b_ref[...]


def _modulation(cond8, w_mod, b_mod):
    n = w_mod.shape[1]
    tn = 1024
    return pl.pallas_call(
        _mod_kernel,
        grid=(n // tn,),
        in_specs=[pl.BlockSpec((8, D_MODEL), lambda j: (0, 0)),
                  pl.BlockSpec((D_MODEL, tn), lambda j: (0, j)),
                  pl.BlockSpec((1, tn), lambda j: (0, j))],
        out_specs=pl.BlockSpec((8, tn), lambda j: (0, j)),
        out_shape=jax.ShapeDtypeStruct((8, n), F32),
        compiler_params=_cparams(("arbitrary",)),
        name="modulation",
    )(cond8, w_mod, b_mod.reshape(1, n))


def _h_kernel(x_ref, mod_ref, g_ref, h_ref, *, grid_rows):
    if grid_rows is None:
        x = x_ref[...]
    else:
        x = x_ref[0].reshape(grid_rows * CM_COLS, D_MODEL)
    y = x * lax.rsqrt(jnp.mean(x * x, axis=-1, keepdims=True) + NORM_EPS) * g_ref[...]
    m = mod_ref[0]
    shift = m[:, :D_MODEL]
    scale = m[:, D_MODEL:2 * D_MODEL]
    h = (y * (1.0 + scale) + shift).astype(BF16)
    if grid_rows is not None:
        n = grid_rows * CM_COLS
        dst = lax.broadcasted_iota(jnp.int32, (n, n), 0)
        src = lax.broadcasted_iota(jnp.int32, (n, n), 1)
        perm = (src == (dst % grid_rows) * CM_COLS + dst // grid_rows).astype(BF16)
        h = _dot(perm, h).astype(BF16)
    h_ref[...] = h


def _adaln_col_major(x, mod3, norm_g, row0):
    b, l, _ = x.shape
    rows = l // GRID_W
    tile = rows * CM_COLS
    return pl.pallas_call(
        functools.partial(_h_kernel, grid_rows=rows),
        grid=(b, GRID_W // CM_COLS),
        in_specs=[pl.BlockSpec((1, rows, CM_COLS, D_MODEL), lambda i, j: (i, 0, j, 0)),
                  pl.BlockSpec((1, 1, 3 * D_MODEL), lambda i, j: (row0 + i, 0, 0)),
                  pl.BlockSpec((1, D_MODEL), lambda i, j: (0, 0))],
        out_specs=pl.BlockSpec((tile, D_MODEL), lambda i, j: (i * (GRID_W // CM_COLS) + j, 0)),
        out_shape=jax.ShapeDtypeStruct((b * l, D_MODEL), BF16),
        compiler_params=_cparams(("arbitrary", "arbitrary")),
        name="adaln_norm_cm",
    )(x.reshape(b, rows, GRID_W, D_MODEL), mod3, norm_g.reshape(1, D_MODEL))


def _adaln(x2, mod3, norm_g, row0, tiles_per_row, tm):
    t = x2.shape[0]
    return pl.pallas_call(
        functools.partial(_h_kernel, grid_rows=None),
        grid=(t // tm,),
        in_specs=[pl.BlockSpec((tm, D_MODEL), lambda i: (i, 0)),
                  pl.BlockSpec((1, 1, 3 * D_MODEL), lambda i: (row0 + i // tiles_per_row, 0, 0)),
                  pl.BlockSpec((1, D_MODEL), lambda i: (0, 0))],
        out_specs=pl.BlockSpec((tm, D_MODEL), lambda i: (i, 0)),
        out_shape=jax.ShapeDtypeStruct((t, D_MODEL), BF16),
        compiler_params=_cparams(("arbitrary",)),
        name="adaln_norm",
    )(x2, mod3, norm_g.reshape(1, D_MODEL))


def _proj_kernel(*refs, mode, tm, seq_len):
    halo = tm != seq_len
    cw, rb_rows = EPI_BLOCK[mode]
    if halo:
        h_ref, hp_ref, hn_ref, w_ref = refs[:4]
        params = refs[4:-1]
        lhs = jnp.concatenate([h_ref[...], hp_ref[...], hn_ref[...]], axis=0)
        start = pl.program_id(0) * tm
        keep_prev = jnp.where((start & (seq_len - 1)) != 0, 1.0, 0.0)
        keep_next = jnp.where(((start + tm) & (seq_len - 1)) != 0, 1.0, 0.0)
    else:
        h_ref, w_ref = refs[:2]
        params = refs[2:-1]
        lhs = h_ref[...]
    o_ref = refs[-1]
    pad = 8
    sub = lax.broadcasted_iota(jnp.int32, (pad, cw), 0)
    nb = rb_rows // pad
    sub3 = lax.broadcasted_iota(jnp.int32, (nb, pad, cw), 1)
    for cb in range(w_ref.shape[1] // cw):
        cols = slice(cb * cw, (cb + 1) * cw)
        p_all = _dot(lhs, w_ref[:, cols])
        if halo:
            top = jnp.where(sub == pad - 1, p_all[tm + HALO - 1:tm + HALO] * keep_prev, 0.0)
            bot = jnp.where(sub == 0, p_all[tm + HALO:tm + HALO + 1] * keep_next, 0.0)
        else:
            top = bot = jnp.zeros((pad, cw), F32)
        ext = jnp.concatenate([top, p_all[:tm], bot], axis=0)
        for rb in range(tm // rb_rows):
            r0 = pad + rb * rb_rows
            win = ext[r0 - pad:r0 + rb_rows + pad].reshape(nb + 2, pad, cw)
            cur = win[1:nb + 1]
            down = pltpu.roll(win[:nb + 1], 1, 1)
            up = pltpu.roll(win[1:], pad - 1, 1)
            prev = jnp.where(sub3 == 0, down[:nb], down[1:])
            nxt = jnp.where(sub3 == pad - 1, up[1:], up[:nb])
            if mode == "conv":
                cw_ref, cb_ref = params
                out = _silu(cw_ref[0:1, cols] * prev + cw_ref[1:2, cols] * cur
                            + cw_ref[2:3, cols] * nxt + cb_ref[:, cols])
            else:
                (mu_ref,) = params
                out = cur + mu_ref[:, cols] * (0.5 * (prev + nxt) - cur)
            o_ref[rb * rb_rows:(rb + 1) * rb_rows, cols] = out.reshape(rb_rows, cw)


EPI_BLOCK = {"conv": (256, 32), "shift": (640, 16)}


def _gates_kernel(h_ref, w_ref, g_ref, dt_ref):
    h = h_ref[...]
    for c0 in range(0, DT_OFF, ACT_COLS):
        p = _dot(h, w_ref[:, c0:c0 + ACT_COLS])
        p = _sigmoid(p) if GL_OFF <= c0 < ZB_OFF else _silu(p)
        g_ref[:, c0:c0 + ACT_COLS] = p.astype(BF16)
    dt_ref[...] = _dot(h, w_ref[:, DT_OFF:])


def _project_gates(h, w, tm=512):
    t, k = h.shape
    return pl.pallas_call(
        _gates_kernel,
        grid=(t // tm,),
        in_specs=[pl.BlockSpec((tm, k), lambda i: (i, 0)),
                  pl.BlockSpec((k, PLAIN_W), lambda i: (0, 0), pipeline_mode=pl.Buffered(1))],
        out_specs=[pl.BlockSpec((tm, DT_OFF), lambda i: (i, 0)),
                   pl.BlockSpec((tm, PLAIN_W - DT_OFF), lambda i: (i, 0))],
        out_shape=[jax.ShapeDtypeStruct((t, DT_OFF), BF16),
                   jax.ShapeDtypeStruct((t, PLAIN_W - DT_OFF), F32)],
        compiler_params=_cparams(("arbitrary",)),
        name="in_proj_gates",
    )(h, w)


def _project(h, w, mode, seq_len, params, tm=512):
    t, k = h.shape
    n = w.shape[1]
    tm = min(tm, seq_len)
    assert seq_len & (seq_len - 1) == 0 and seq_len % tm == 0 and tm % HALO == 0
    in_specs = [pl.BlockSpec((tm, k), lambda i: (i, 0))]
    args = [h]
    if tm != seq_len:
        per = tm // HALO
        in_specs += [pl.BlockSpec((HALO, k), lambda i: (jnp.maximum(i * per - 1, 0), 0)),
                     pl.BlockSpec((HALO, k), lambda i: (jnp.minimum((i + 1) * per, t // HALO - 1), 0))]
        args += [h, h]
    in_specs.append(pl.BlockSpec((k, n), lambda i: (0, 0), pipeline_mode=pl.Buffered(1)))
    args.append(w)
    for prm in params:
        in_specs.append(pl.BlockSpec(prm.shape, lambda i: (0, 0)))
        args.append(prm)
    return pl.pallas_call(
        functools.partial(_proj_kernel, mode=mode, tm=tm, seq_len=seq_len),
        grid=(t // tm,),
        in_specs=in_specs,
        out_specs=pl.BlockSpec((tm, n), lambda i: (i, 0)),
        out_shape=jax.ShapeDtypeStruct((t, n), F32),
        compiler_params=_cparams(("arbitrary",)),
        name="in_proj_" + mode,
    )(*args)


def _ssd_kernel(*refs, rev, has_init, has_prev, n_sub):
    refs = list(refs)
    xs_ref, b_ref, c_ref, cs_ref, cst_ref, dtt_ref, ddt_ref = refs[:7]
    pos = 7
    s0_ref = None
    if has_init:
        s0_ref = refs[pos]
        pos += 1
    yprev_ref = dskip_ref = None
    if has_prev:
        yprev_ref, dskip_ref = refs[pos], refs[pos + 1]
        pos += 2
    y_ref, sfin_ref, s_ref = refs[pos], refs[pos + 1], refs[pos + 2]

    q = SSM_CHUNK
    c = pl.program_id(1)

    @pl.when(c == 0)
    def _():
        if has_init:
            s_ref[...] = s0_ref[0]
        else:
            s_ref[...] = jnp.zeros_like(s_ref)

    ii = lax.broadcasted_iota(jnp.int32, (q, q), 0)
    jj = lax.broadcasted_iota(jnp.int32, (q, q), 1)
    incl = (jj >= ii) if rev else (jj <= ii)
    lane_lo = jj < SSM_HEADDIM
    row_lo = ii < SSM_HEADDIM
    last = 0 if rev else q - 1

    m_lo = lane_lo.astype(BF16)
    m_hi = jnp.logical_not(lane_lo).astype(BF16)
    subs = list(range(n_sub))[::-1] if rev else list(range(n_sub))
    for j, g in [(j, g) for j in subs for g in range(SSM_GROUPS)]:
        rows = slice(j * q, (j + 1) * q)
        cs = cs_ref[0, 0, rows, :]
        cs_t = cst_ref[0, 0, j]
        dt_t = dtt_ref[0, 0, j]
        dd_t = ddt_ref[0, 0, j]
        bm = b_ref[0, rows, g * SSM_STATE:(g + 1) * SSM_STATE].astype(BF16)
        cm = c_ref[0, rows, g * SSM_STATE:(g + 1) * SSM_STATE].astype(BF16)
        cb = _dot_nt(cm, bm)
        ps = [g * (SSM_PAIRS // SSM_GROUPS) + jp for jp in range(SSM_PAIRS // SSM_GROUPS)]
        idx = range(len(ps))
        lanes = [slice(p * LANES, (p + 1) * LANES) for p in ps]
        xs = [xs_ref[0, rows, lanes[i]] for i in idx]
        cs_col = {h: cs[:, h:h + 1] for p in ps for h in (2 * p, 2 * p + 1)}
        cs_pair = [jnp.where(lane_lo, cs_col[2 * p], cs_col[2 * p + 1]) for p in ps]
        w_pair = []
        for p in ps:
            w_heads = []
            for h in (2 * p, 2 * p + 1):
                seg = cs_col[h] - cs_t[h:h + 1, :]
                lm = jnp.where(incl, jnp.exp(seg), 0.0)
                w_heads.append((cb * lm * dt_t[h:h + 1, :]).astype(BF16))
            w_pair.append(jnp.concatenate(w_heads, axis=1))
        xs_b = [q_.astype(BF16) for q_ in xs]
        y_diag = [_dot(w_pair[i], jnp.concatenate([xs_b[i] * m_lo, xs_b[i] * m_hi], axis=0)) for i in idx]
        s_pair = [s_ref[p] for p in ps]
        y_off = [_dot_nt(cm, s_pair[i].astype(BF16)) * jnp.exp(cs_pair[i]) for i in idx]
        for i in idx:
            y = y_diag[i] + y_off[i]
            if has_prev:
                y = y + yprev_ref[0, rows, lanes[i]] + dskip_ref[:, lanes[i]] * xs[i]
            y_ref[0, rows, lanes[i]] = y
        scale = [jnp.where(row_lo, dd_t[2 * p:2 * p + 1, :], dd_t[2 * p + 1:2 * p + 2, :]) for p in ps]
        upd = [_dot((xs[i].T * scale[i]).astype(BF16), bm) for i in idx]
        for i, p in enumerate(ps):
            end_col = jnp.where(row_lo, cs_t[2 * p:2 * p + 1, last:last + 1], cs_t[2 * p + 1:2 * p + 2, last:last + 1])
            s_ref[p] = s_pair[i] * jnp.exp(end_col) + upd[i]

    @pl.when(c == pl.num_programs(1) - 1)
    def _():
        sfin_ref[0] = s_ref[...]


def _ssd_prep_kernel(raw_ref, alog_ref, dtb_ref, cs_ref, cst_ref, dtt_ref, ddt_ref, *, nchunks):
    q = SSM_CHUNK
    ii = lax.broadcasted_iota(jnp.int32, (q, q), 0)
    jj = lax.broadcasted_iota(jnp.int32, (q, q), 1)
    tri = [[(jj <= ii).astype(BF16)], [(jj >= ii).astype(BF16)]]
    last = [q - 1, 0]
    chains = [(d, ck) for d in range(2) for ck in range(nchunks)]
    rows = [slice(ck * q, (ck + 1) * q) for _, ck in chains]
    dt = [_softplus(raw_ref[0, rows[i], :] + dtb_ref[d:d + 1, :]) for i, (d, _) in enumerate(chains)]
    a = [dt[i] * (-jnp.exp(alog_ref[d:d + 1, :])) for i, (d, _) in enumerate(chains)]
    cs = [_mm(tri[d], _pieces(a[i], 3)) for i, (d, _) in enumerate(chains)]
    dd = [dt[i] * jnp.exp(cs[i][last[d]:last[d] + 1, :] - cs[i]) for i, (d, _) in enumerate(chains)]
    for i, (d, ck) in enumerate(chains):
        cs_ref[d, 0, rows[i], :] = cs[i]
        cst_ref[d, 0, ck] = cs[i].T
        dtt_ref[d, 0, ck] = dt[i].T
        ddt_ref[d, 0, ck] = dd[i].T


def _ssd_prep(p_dt3, alog2, dtb2):
    b, l, _ = p_dt3.shape
    rows = min(l, 8 * SSM_CHUNK)
    nck = rows // SSM_CHUNK
    dt_blk = 0
    t_spec = pl.BlockSpec((2, 1, nck, SSM_CHUNK, LANES), lambda i, j: (0, i, j, 0, 0))
    t_shape = jax.ShapeDtypeStruct((2, b, l // SSM_CHUNK, SSM_CHUNK, LANES), F32)
    return pl.pallas_call(
        functools.partial(_ssd_prep_kernel, nchunks=nck),
        grid=(b, l // rows),
        in_specs=[pl.BlockSpec((1, rows, LANES), lambda i, j: (i, j, dt_blk)),
                  pl.BlockSpec((2, LANES), lambda i, j: (0, 0)),
                  pl.BlockSpec((2, LANES), lambda i, j: (0, 0))],
        out_specs=[pl.BlockSpec((2, 1, rows, LANES), lambda i, j: (0, i, j, 0)), t_spec, t_spec, t_spec],
        out_shape=[jax.ShapeDtypeStruct((2, b, l, LANES), F32), t_shape, t_shape, t_shape],
        compiler_params=_cparams(("arbitrary", "arbitrary")),
        name="ssd_prep",
    )(p_dt3, alog2, dtb2)


def _ssd_scan(xbc, prep, s0, yprev, dskip, rev):
    b, l, _ = xbc.shape
    n_sub = min(SSD_SUB, l // SSM_CHUNK)
    q = SSM_CHUNK * n_sub
    nc = l // q
    cidx = (lambda c: nc - 1 - c) if rev else (lambda c: c)
    d = 1 if rev else 0
    t_spec = pl.BlockSpec((1, 1, n_sub, SSM_CHUNK, LANES), lambda i, c: (d, i, cidx(c), 0, 0))
    in_specs = [pl.BlockSpec((1, q, SSM_D_INNER), lambda i, c: (i, cidx(c), 0)),
                pl.BlockSpec((1, q, 512), lambda i, c: (i, cidx(c), SSM_D_INNER // 512)),
                pl.BlockSpec((1, q, 512), lambda i, c: (i, cidx(c), SSM_D_INNER // 512 + 1)),
                pl.BlockSpec((1, 1, q, LANES), lambda i, c: (d, i, cidx(c), 0)),
                t_spec, t_spec, t_spec]
    args = [xbc, xbc, xbc, *prep]
    if s0 is not None:
        in_specs.append(pl.BlockSpec((1, SSM_PAIRS, LANES, SSM_STATE), lambda i, c: (i, 0, 0, 0)))
        args.append(s0)
    if yprev is not None:
        in_specs.append(pl.BlockSpec((1, q, SSM_D_INNER), lambda i, c: (i, cidx(c), 0)))
        in_specs.append(pl.BlockSpec((1, SSM_D_INNER), lambda i, c: (0, 0)))
        args += [yprev, dskip]
    kern = functools.partial(_ssd_kernel, rev=rev, has_init=s0 is not None, has_prev=yprev is not None, n_sub=n_sub)
    return pl.pallas_call(
        kern,
        grid=(b, nc),
        in_specs=in_specs,
        out_specs=[pl.BlockSpec((1, q, SSM_D_INNER), lambda i, c: (i, cidx(c), 0)),
                   pl.BlockSpec((1, SSM_PAIRS, LANES, SSM_STATE), lambda i, c: (i, 0, 0, 0))],
        out_shape=[jax.ShapeDtypeStruct((b, l, SSM_D_INNER), F32),
                   jax.ShapeDtypeStruct((b, SSM_PAIRS, LANES, SSM_STATE), F32)],
        scratch_shapes=[pltpu.VMEM((SSM_PAIRS, LANES, SSM_STATE), F32)],
        compiler_params=_cparams(("arbitrary", "arbitrary")),
        name="ssd_bwd" if rev else "ssd_fwd",
    )(*args)


CUM_PIECES = 2


def _pieces(x, n):
    out = []
    for i in range(n):
        h = x.astype(BF16)
        out.append(h)
        if i + 1 < n:
            x = x - h.astype(F32)
    return out


def _mm(a_pieces, b_pieces, nt=False):
    dot = _dot_nt if nt else _dot
    depth = max(len(a_pieces), len(b_pieces))
    acc = None
    for i in reversed(range(len(a_pieces))):
        for j in reversed(range(len(b_pieces))):
            if i + j < depth:
                t = dot(a_pieces[i], b_pieces[j])
                acc = t if acc is None else acc + t
    return acc


def _block_diag(y, m_lo, m_hi):
    return jnp.concatenate([y * m_lo, y * m_hi], axis=0)


def _wkv_kernel(*refs, rev, has_init, final, grid_rows):
    refs = list(refs)
    (rw_ref, w0_ref, w2_ref, a0_ref, a2_ref, kk_ref, ka_ref) = refs[:7]
    pos = 7
    s0_ref = None
    if has_init:
        s0_ref = refs[pos]
        pos += 1
    of_ref = rk_ref = lnw_ref = lnb_ref = None
    if final:
        of_ref, rk_ref, lnw_ref, lnb_ref = refs[pos:pos + 4]
        pos += 4
    o_ref, sfin_ref, s_ref = refs[pos], refs[pos + 1], refs[pos + 2]

    n = WKV_CHUNK
    c = pl.program_id(1)

    @pl.when(c == 0)
    def _():
        if has_init:
            s_ref[...] = s0_ref[0]
        else:
            s_ref[...] = jnp.zeros_like(s_ref)

    xl = rw_ref[0, :, 3 * WKV_WIDTH:3 * WKV_WIDTH + LANES]
    a_full = _sigmoid(a0_ref[...] + _dot(xl.astype(BF16), a2_ref[...]))
    xw = w0_ref[...] + _dot(jnp.tanh(xl).astype(BF16), w2_ref[...])
    lw_full = -math.exp(-0.5) * _sigmoid(xw)

    ti = lax.broadcasted_iota(jnp.int32, (n, LANES), 0)
    li = lax.broadcasted_iota(jnp.int32, (n, LANES), 1)
    si = li & (WKV_HEADSIZE - 1)
    strict = (si > ti) if rev else (si < ti)
    incl = (si >= ti) if rev else (si <= ti)
    eye = (si == ti).astype(BF16)
    m_lo = (li < WKV_HEADSIZE).astype(BF16)
    m_hi = (li >= WKV_HEADSIZE).astype(BF16)
    tq = lax.broadcasted_iota(jnp.int32, (n, n), 0)
    sq = lax.broadcasted_iota(jnp.int32, (n, n), 1)
    tri = jnp.concatenate([((sq >= tq) if rev else (sq <= tq)).astype(BF16)] * CUM_PIECES, axis=1)
    r2 = lax.broadcasted_iota(jnp.int32, (LANES, LANES), 0)
    c2 = lax.broadcasted_iota(jnp.int32, (LANES, LANES), 1)
    same_head = (r2 < WKV_HEADSIZE) == (c2 < WKV_HEADSIZE)
    ones_bd = same_head.astype(BF16)
    last = 0 if rev else n - 1

    def level_mask(m):
        same = (ti >> (m.bit_length())) == (si >> (m.bit_length()))
        t_hi = (ti & m) != 0
        s_hi = (si & m) != 0
        if rev:
            return same & jnp.logical_not(t_hi) & s_hi
        return same & t_hi & jnp.logical_not(s_hi)

    levels = []
    m = 2
    while m < n:
        levels.append(level_mask(m).astype(BF16))
        m *= 2
    level1 = level_mask(1).astype(BF16)

    def bd(y):
        return _block_diag(y.astype(BF16), m_lo, m_hi)

    def seg_sum(x):
        return _dot(x.astype(BF16), ones_bd)

    def lane_blk(p, base=0):
        return slice(base + p * LANES, base + (p + 1) * LANES)

    def rows(j):
        return slice(j * n, (j + 1) * n)

    subs = list(range(WKV_SUB))[::-1] if rev else list(range(WKV_SUB))
    chains = [(j, p) for j in subs for p in range(N_PAIRS)]
    idx = range(len(chains))
    r = [rw_ref[0, rows(j), lane_blk(p)] for j, p in chains]
    k = [rw_ref[0, rows(j), lane_blk(p, WKV_WIDTH)] for j, p in chains]
    v = [rw_ref[0, rows(j), lane_blk(p, 2 * WKV_WIDTH)] for j, p in chains]
    a = [a_full[rows(j), lane_blk(p)] for j, p in chains]
    lw = [lw_full[rows(j), lane_blk(p)] for j, p in chains]
    kkr = [k[i] * kk_ref[p] for i, (j, p) in enumerate(chains)]
    ss = [seg_sum(q * q) for q in kkr]
    kk = [kkr[i] / jnp.maximum(jnp.sqrt(ss[i]), 1e-12) for i in idx]
    kmod = [k[i] * (1.0 + (a[i] - 1.0) * ka_ref[p]) for i, (j, p) in enumerate(chains)]
    kka = [kk[i] * a[i] for i in idx]

    lg = [_dot(tri, jnp.concatenate(_pieces(q, CUM_PIECES), axis=0)) for q in lw]
    g_inv = [jnp.exp(-q) for q in lg]
    rh = [r[i] * jnp.exp(lg[i]) for i in idx]
    kh = [kmod[i] * g_inv[i] for i in idx]
    ah = [kka[i] * g_inv[i] for i in idx]
    bh = [-kk[i] * jnp.exp(lg[i] - lw[i]) for i in idx]
    g_end = [jnp.exp(q[last:last + 1, :]) for q in lg]

    lhs = [jnp.concatenate([bh[i], rh[i]], axis=0).astype(BF16) for i in idx]
    rhs = [jnp.concatenate([bd(ah[i]), bd(kh[i])], axis=0) for i in idx]
    aak = [_dot_nt(lhs[i], rhs[i]) for i in idx]
    a_ab = [jnp.where(strict, q[:n, :LANES], 0.0).astype(BF16) for q in aak]
    a_ra = [jnp.where(incl, q[n:, :LANES], 0.0).astype(BF16) for q in aak]
    a_bkrk = [jnp.concatenate([jnp.where(strict, q[:n, LANES:], 0.0), jnp.where(incl, q[n:, LANES:], 0.0)],
                              axis=0).astype(BF16) for q in aak]

    x = [eye + q * level1 for q in a_ab]
    for lm in levels:
        pm = [_dot(x[i], bd(a_ab[i] * lm)).astype(BF16) for i in idx]
        x = [x[i] + _dot(pm[i], bd(x[i])).astype(BF16) for i in idx]
    gv = [_dot(a_bkrk[i], bd(v[i])) for i in idx]
    ake = [(jnp.concatenate([ah[i], kh[i]], axis=0) * g_end[i]).astype(BF16) for i in idx]

    o = [None] * len(chains)
    for jj in range(WKV_SUB):
        ids = list(range(jj * N_PAIRS, (jj + 1) * N_PAIRS))
        s_bd = {i: s_ref[chains[i][1]] for i in ids}
        x0r = {i: _dot_nt(lhs[i], s_bd[i].astype(BF16)) for i in ids}
        u = {i: _dot(x[i], bd(x0r[i][:n] + gv[i][:n])) for i in ids}
        for i in ids:
            o[i] = x0r[i][n:] + gv[i][n:] + _dot(a_ra[i], bd(u[i]))
        uvt = {i: jnp.concatenate([u[i], v[i]], axis=0).T.astype(BF16) for i in ids}
        upd = {i: _dot(uvt[i], ake[i]) for i in ids}
        for i in ids:
            s_ref[chains[i][1]] = s_bd[i] * g_end[i] + jnp.where(same_head, upd[i], 0.0)

    if final:
        o = [o[i] + of_ref[0, p, rows(j), :] for i, (j, p) in enumerate(chains)]
        mu = [seg_sum(q) * (1.0 / WKV_HEADSIZE) for q in o]
        d = [o[i] - mu[i] for i in idx]
        var = [seg_sum(q * q) * (1.0 / WKV_HEADSIZE) for q in d]
        bonus = [seg_sum(r[i] * kmod[i] * rk_ref[p]) * v[i] for i, (j, p) in enumerate(chains)]
        o = [d[i] * lax.rsqrt(var[i] + WKV_GN_EPS) * lnw_ref[p] + lnb_ref[p] + bonus[i]
             for i, (j, p) in enumerate(chains)]
    if grid_rows is None:
        for i, (j, p) in enumerate(chains):
            o_ref[0, p, rows(j), :] = o[i]
    else:
        step = (pl.num_programs(1) - 1 - c) if rev else c
        cols_per_sub = n // grid_rows
        for i, (j, p) in enumerate(chains):
            for wl in range(cols_per_sub):
                w = (step * WKV_SUB + j) * cols_per_sub + wl
                o_ref[0, p, pl.ds(w, grid_rows, stride=GRID_W), :] = o[i][wl * grid_rows:(wl + 1) * grid_rows]

    @pl.when(c == pl.num_programs(1) - 1)
    def _():
        sfin_ref[0] = s_ref[...]


def _pairs(vec):
    return vec.reshape(N_PAIRS, 1, LANES)


def _wkv_scan(rw_s, w0, w2p, a0, a2p, k_k, k_a, s0, o_f, r_k, ln_w, ln_b, rev, to_row_major=False):
    b, l, w = rw_s.shape
    n = WKV_CHUNK * WKV_SUB
    nc = l // n
    cidx = (lambda c: nc - 1 - c) if rev else (lambda c: c)
    final = o_f is not None
    vec_spec = pl.BlockSpec((N_PAIRS, 1, LANES), lambda i, c: (0, 0, 0))
    st_spec = pl.BlockSpec((1, N_PAIRS, LANES, LANES), lambda i, c: (i, 0, 0, 0))
    in_specs = [pl.BlockSpec((1, n, w), lambda i, c: (i, cidx(c), 0)),
                pl.BlockSpec((1, WKV_WIDTH), lambda i, c: (0, 0)),
                pl.BlockSpec((LANES, WKV_WIDTH), lambda i, c: (0, 0)),
                pl.BlockSpec((1, WKV_WIDTH), lambda i, c: (0, 0)),
                pl.BlockSpec((LANES, WKV_WIDTH), lambda i, c: (0, 0)),
                vec_spec, vec_spec]
    args = [rw_s, w0.reshape(1, WKV_WIDTH), w2p, a0.reshape(1, WKV_WIDTH), a2p, _pairs(k_k), _pairs(k_a)]
    if s0 is not None:
        in_specs.append(st_spec)
        args.append(s0)
    if final:
        in_specs += [pl.BlockSpec((1, N_PAIRS, n, LANES), lambda i, c: (i, 0, cidx(c), 0)),
                     vec_spec, vec_spec, vec_spec]
        args += [o_f, _pairs(r_k), _pairs(ln_w), _pairs(ln_b)]
    o_spec = pl.BlockSpec((1, N_PAIRS, n, LANES), lambda i, c: (i, 0, cidx(c), 0))
    grid_rows = None
    if to_row_major:
        o_spec = pl.BlockSpec((1, N_PAIRS, l, LANES), lambda i, c: (i, 0, 0, 0))
        grid_rows = l // GRID_W
        assert WKV_CHUNK % grid_rows == 0
    kern = functools.partial(_wkv_kernel, rev=rev, has_init=s0 is not None, final=final, grid_rows=grid_rows)
    return pl.pallas_call(
        kern,
        grid=(b, nc),
        in_specs=in_specs,
        out_specs=[o_spec, st_spec],
        out_shape=[jax.ShapeDtypeStruct((b, N_PAIRS, l, LANES), F32),
                   jax.ShapeDtypeStruct((b, N_PAIRS, LANES, LANES), F32)],
        scratch_shapes=[pltpu.VMEM((N_PAIRS, LANES, LANES), F32)],
        compiler_params=_cparams(("arbitrary", "arbitrary")),
        name="wkv_bwd" if rev else "wkv_fwd",
    )(*args)


def _final_kernel(y_ref, za_ref, o_ref, zb_ref, ga_ref, gb_ref, x_ref, mod_ref,
                  ng_ref, pa_ref, pb_ref, wo_ref, fg_ref, out_ref):
    y = y_ref[...] * za_ref[...].astype(F32)
    y = y * lax.rsqrt(jnp.mean(y * y, axis=-1, keepdims=True) + NORM_EPS) * ng_ref[...]
    u_a = _dot(y.astype(BF16), pa_ref[...])
    o = jnp.concatenate([jnp.concatenate([o_ref[s, p] for p in range(N_PAIRS)], axis=1)
                         for s in range(o_ref.shape[0])], axis=0)
    u_b = _dot((o * zb_ref[...].astype(F32)).astype(BF16), pb_ref[...])
    m = ga_ref[...].astype(F32) * u_a + gb_ref[...].astype(F32) * u_b
    out = _dot(m.astype(BF16), wo_ref[...])
    gate = mod_ref[0][:, 2 * D_MODEL:]
    xo = x_ref[...] + gate * out
    out_ref[...] = xo * lax.rsqrt(jnp.mean(xo * xo, axis=-1, keepdims=True) + NORM_EPS) * fg_ref[...]


def _final(y2, p_plain, o4, x2, mod3, ssm_norm_g, p_a, p_b, w_out, final_g, row0, tiles_per_row, tm):
    t = x2.shape[0]
    w1 = D_MODEL
    l = o4.shape[2]
    o_rows = min(tm, l)
    tiles_per_seq = l // o_rows

    def resident(shape):
        return pl.BlockSpec(shape, lambda i: (0, 0), pipeline_mode=pl.Buffered(1))

    return pl.pallas_call(
        _final_kernel,
        grid=(t // tm,),
        in_specs=[pl.BlockSpec((tm, SSM_D_INNER), lambda i: (i, 0)),
                  pl.BlockSpec((tm, SSM_D_INNER), lambda i: (i, 0)),
                  pl.BlockSpec((tm // o_rows, N_PAIRS, o_rows, LANES),
                               lambda i: (i // tiles_per_seq, 0, i % tiles_per_seq, 0)),
                  pl.BlockSpec((tm, w1), lambda i: (i, ZB_OFF // w1)),
                  pl.BlockSpec((tm, w1), lambda i: (i, GL_OFF // w1)),
                  pl.BlockSpec((tm, w1), lambda i: (i, GL_OFF // w1 + 1)),
                  pl.BlockSpec((tm, w1), lambda i: (i, 0)),
                  pl.BlockSpec((1, 1, 3 * D_MODEL), lambda i: (row0 + i // tiles_per_row, 0, 0)),
                  resident((1, SSM_D_INNER)),
                  resident((SSM_D_INNER, D_MODEL)),
                  resident((WKV_WIDTH, D_MODEL)),
                  resident((D_MODEL, D_MODEL)),
                  resident((1, D_MODEL))],
        out_specs=pl.BlockSpec((tm, D_MODEL), lambda i: (i, 0)),
        out_shape=jax.ShapeDtypeStruct((t, D_MODEL), F32),
        compiler_params=_cparams(("arbitrary",), MERGE_VMEM_LIMIT),
        name="merge_out",
    )(y2, p_plain, o4, p_plain, p_plain, p_plain, x2, mod3,
      ssm_norm_g.reshape(1, SSM_D_INNER), p_a, p_b, w_out, final_g.reshape(1, D_MODEL))


def _wkv_state_to_pairs(s):
    b = s.shape[0]
    s = s.reshape(b, N_PAIRS, 2, WKV_HEADSIZE, WKV_HEADSIZE)
    z = jnp.zeros_like(s[:, :, 0])
    top = jnp.concatenate([s[:, :, 0], z], axis=-1)
    bot = jnp.concatenate([z, s[:, :, 1]], axis=-1)
    return jnp.concatenate([top, bot], axis=-2)


def _wkv_state_from_pairs(s):
    b = s.shape[0]
    h = WKV_HEADSIZE
    return jnp.stack([s[:, :, :h, :h], s[:, :, h:, h:]], axis=2).reshape(b, WKV_HEADS, h, h)


def _group(x, mod3, row0, grid, states, wts):
    b, l, _ = x.shape
    t = b * l
    x2 = x.reshape(t, D_MODEL)
    h = _adaln(x2, mod3, wts["norm_g"], row0, (l if grid else t) // ADALN_TM, ADALN_TM)
    p_plain, p_dt = _project_gates(h, wts["w_plain"])
    p_dt3 = p_dt.reshape(b, l, PLAIN_W - DT_OFF)
    h_rw = _adaln_col_major(x, mod3, wts["norm_g"], row0) if grid else h

    xbc = _project(h, wts["w_xbc"], "conv", l,
                   (wts["conv_w"], wts["conv_b"].reshape(1, XBC_W))).reshape(b, l, XBC_W)
    s_f = s_b = None
    if states is not None:
        s_f = states[0].reshape(b, SSM_PAIRS, LANES, SSM_STATE)
        s_b = states[1].reshape(b, SSM_PAIRS, LANES, SSM_STATE)
    prep = _ssd_prep(p_dt3, wts["alog"], wts["dtb"])
    y_f, fs_f = _ssd_scan(xbc, prep, s_f, None, None, rev=False)
    y, fs_b = _ssd_scan(xbc, prep, s_b, y_f, wts["dskip"], rev=True)

    rw_s = _project(h_rw, wts["w_rw"], "shift", l,
                    (wts["shift_mu"].reshape(1, RWKV_SCAN_W),)).reshape(b, l, RWKV_SCAN_W)
    w_f = w_b = None
    if states is not None:
        w_f = _wkv_state_to_pairs(states[2])
        w_b = _wkv_state_to_pairs(states[3])
    o_f, fw_f = _wkv_scan(rw_s, wts["w0"][0], wts["w2p"][0], wts["a0"], wts["a2p"], wts["k_k"], wts["k_a"],
                          w_f, None, None, None, None, rev=False)
    o, fw_b = _wkv_scan(rw_s, wts["w0"][1], wts["w2p"][1], wts["a0"], wts["a2p"], wts["k_k"], wts["k_a"],
                        w_b, o_f, wts["r_k"], wts["ln_w"], wts["ln_b"], rev=True, to_row_major=grid)

    out = _final(y.reshape(t, SSM_D_INNER), p_plain, o, x2, mod3, wts["ssm_norm_g"],
                 wts["p_a"], wts["p_b"], wts["w_out"], wts["final_g"], row0,
                 (l if grid else t) // FINAL_TM, FINAL_TM)
    finals = (fs_f.reshape(b, SSM_HEADS, SSM_HEADDIM, SSM_STATE), fs_b.reshape(b, SSM_HEADS, SSM_HEADDIM, SSM_STATE),
              _wkv_state_from_pairs(fw_f), _wkv_state_from_pairs(fw_b))
    return out.reshape(b, l, D_MODEL), finals


def kernel(x_prompt, x_sample, state_ssm_fwd, state_ssm_bwd, state_wkv_fwd, state_wkv_bwd, c, c_ctx, w_mod, b_mod, norm_g, w_in, conv_w, conv_b, a_log, dt_bias, d_skip, ssm_norm_g, p_a, shift_mu, w0, w2, a0, a2, k_k, k_a, r_k, ln_w, ln_b, p_b, w_out, final_g):
    depth = w_mod.shape[0]
    assert depth == 1, "single-layer stack only"
    l0 = 0
    w_in0 = w_in[l0].astype(BF16)
    zpad = jnp.zeros((WKV_RANK, WKV_WIDTH), F32)
    w_plain = jnp.concatenate([w_in0[:, :ZA_END], w_in0[:, ZB_END:], w_in0[:, RW_END:ZB_END], w_in0[:, XBC_END:DT_END],
                               jnp.zeros((D_MODEL, PLAIN_W - DT_OFF - SSM_HEADS), BF16)], axis=1)
    wts = {
        "norm_g": norm_g[l0],
        "w_plain": w_plain,
        "w_xbc": w_in0[:, ZA_END:XBC_END],
        "w_rw": w_in0[:, DT_END:RW_END],
        "conv_w": conv_w[l0], "conv_b": conv_b[l0],
        "alog": jnp.pad(a_log[l0], ((0, 0), (0, LANES - SSM_HEADS))),
        "dtb": jnp.pad(dt_bias[l0], ((0, 0), (0, LANES - SSM_HEADS))),
        "dskip": jnp.repeat(d_skip[l0], SSM_HEADDIM).reshape(1, SSM_D_INNER),
        "ssm_norm_g": ssm_norm_g[l0],
        "p_a": p_a[l0].astype(BF16), "p_b": p_b[l0].astype(BF16), "w_out": w_out[l0].astype(BF16),
        "shift_mu": shift_mu[l0],
        "w0": w0[l0],
        "w2p": [jnp.concatenate([w2[l0, d], zpad], axis=0).astype(BF16) for d in range(2)],
        "a0": a0[l0],
        "a2p": jnp.concatenate([zpad, a2[l0]], axis=0).astype(BF16),
        "k_k": k_k[l0], "k_a": k_a[l0], "r_k": r_k[l0], "ln_w": ln_w[l0], "ln_b": ln_b[l0],
        "final_g": final_g,
    }
    nb = c.shape[0]
    cond8 = jnp.concatenate([c_ctx[None, :], c, jnp.zeros((8 - 1 - nb, D_MODEL), F32)], axis=0)
    mod3 = _modulation(cond8, w_mod[l0], b_mod[l0]).reshape(8, 1, 3 * D_MODEL)

    y_prompt, (sf, sb, wf, wb) = _group(x_prompt, mod3, 0, False, None, wts)
    lat_states = (state_ssm_fwd[:, l0], state_ssm_bwd[:, l0], state_wkv_fwd[:, l0], state_wkv_bwd[:, l0])
    y_sample, _ = _group(x_sample, mod3, 1, True, lat_states, wts)
    return (y_prompt, y_sample, sf[:, None], sb[:, None], wf[:, None], wb[:, None])
```

```python
import functools
import math

import jax
import jax.numpy as jnp
from jax import lax
from jax.experimental import pallas as pl
from jax.experimental.pallas import tpu as pltpu

F32 = jnp.float32
BF16 = jnp.bfloat16
HI = lax.Precision.HIGHEST

D_MODEL = 1024
GRID_W = 64
NORM_EPS = 1e-6
SSM_D_INNER = 2048
SSM_HEADDIM = 64
SSM_HEADS = 32
SSM_GROUPS = 4
SSM_STATE = 128
SSM_CHUNK = 128
WKV_WIDTH = 1024
WKV_HEADSIZE = 64
WKV_HEADS = 16
WKV_RANK = 64
WKV_GN_EPS = 64e-5
WKV_CHUNK = 64
XBC_W = SSM_D_INNER + 2 * SSM_GROUPS * SSM_STATE
RWKV_SCAN_W = 3 * WKV_WIDTH + 2 * WKV_RANK
ZA_END = SSM_D_INNER
XBC_END = ZA_END + XBC_W
DT_END = XBC_END + SSM_HEADS
RW_END = DT_END + RWKV_SCAN_W
ZB_END = RW_END + WKV_WIDTH
IN_W = ZB_END + 2 * D_MODEL

LANES = 128
N_PAIRS = WKV_HEADS // 2
SSM_PAIRS = SSM_HEADS // 2
WKV_SUB = 4
SSD_SUB = 4
FINAL_TM = 512
ADALN_TM = 1024
CM_COLS = 16
HALO = 16
GL_OFF = SSM_D_INNER
ZB_OFF = GL_OFF + 2 * D_MODEL
DT_OFF = ZB_OFF + WKV_WIDTH
ACT_COLS = 2 * LANES
PLAIN_W = 21 * ACT_COLS
VMEM_LIMIT = 48 * 1024 * 1024
MERGE_VMEM_LIMIT = 56 * 1024 * 1024


def _cparams(sem, vmem_limit=VMEM_LIMIT):
    return pltpu.CompilerParams(dimension_semantics=sem, vmem_limit_bytes=vmem_limit)


def _sigmoid(x):
    return 1.0 / (1.0 + jnp.exp2(x * (-1.0 / math.log(2.0))))


def _silu(x):
    return x * _sigmoid(x)


def _softplus(x):
    return jnp.maximum(x, 0.0) + jnp.log(1.0 + jnp.exp(-jnp.abs(x)))


def _dot(a, b, precision=None):
    return jnp.dot(a, b, preferred_element_type=F32, precision=precision)


def _dot_nt(a, b, precision=None):
    return lax.dot_general(a, b, (((1,), (1,)), ((), ())), preferred_element_type=F32, precision=precision)


def _mod_kernel(c_ref, w_ref, b_ref, o_ref):
    c = c_ref[...]
    o_ref[...] = _dot(_silu(c), w_ref[...], HI) + b_ref[...]


def _modulation(cond8, w_mod, b_mod):
    n = w_mod.shape[1]
    tn = 1024
    return pl.pallas_call(
        _mod_kernel,
        grid=(n // tn,),
        in_specs=[pl.BlockSpec((8, D_MODEL), lambda j: (0, 0)),
                  pl.BlockSpec((D_MODEL, tn), lambda j: (0, j)),
                  pl.BlockSpec((1, tn), lambda j: (0, j))],
        out_specs=pl.BlockSpec((8, tn), lambda j: (0, j)),
        out_shape=jax.ShapeDtypeStruct((8, n), F32),
        compiler_params=_cparams(("arbitrary",)),
        name="modulation",
    )(cond8, w_mod, b_mod.reshape(1, n))


def _h_kernel(x_ref, mod_ref, g_ref, h_ref, *maybe_cm_ref, grid_rows):
    if grid_rows is None:
        x = x_ref[...]
    else:
        x = x_ref[0].reshape(grid_rows * CM_COLS, D_MODEL)
    y = x * lax.rsqrt(jnp.mean(x * x, axis=-1, keepdims=True) + NORM_EPS) * g_ref[...]
    m = mod_ref[0]
    shift = m[:, :D_MODEL]
    scale = m[:, D_MODEL:2 * D_MODEL]
    h = (y * (1.0 + scale) + shift).astype(BF16)
    if grid_rows is None:
        h_ref[...] = h
        return
    h_ref[0] = h.reshape(grid_rows, CM_COLS, D_MODEL)
    (cm_ref,) = maybe_cm_ref
    n = grid_rows * CM_COLS
    dst = lax.broadcasted_iota(jnp.int32, (n, n), 0)
    src = lax.broadcasted_iota(jnp.int32, (n, n), 1)
    perm = (src == (dst % grid_rows) * CM_COLS + dst // grid_rows).astype(BF16)
    cm_ref[...] = _dot(perm, h).astype(BF16)


def _adaln_grid(x, mod3, norm_g, row0):
    b, l, _ = x.shape
    rows = l // GRID_W
    tile = rows * CM_COLS
    x_spec = pl.BlockSpec((1, rows, CM_COLS, D_MODEL), lambda i, j: (i, 0, j, 0))
    h, h_cm = pl.pallas_call(
        functools.partial(_h_kernel, grid_rows=rows),
        grid=(b, GRID_W // CM_COLS),
        in_specs=[x_spec,
                  pl.BlockSpec((1, 1, 3 * D_MODEL), lambda i, j: (row0 + i, 0, 0)),
                  pl.BlockSpec((1, D_MODEL), lambda i, j: (0, 0))],
        out_specs=[x_spec, pl.BlockSpec((tile, D_MODEL), lambda i, j: (i * (GRID_W // CM_COLS) + j, 0))],
        out_shape=[jax.ShapeDtypeStruct((b, rows, GRID_W, D_MODEL), BF16),
                   jax.ShapeDtypeStruct((b * l, D_MODEL), BF16)],
        compiler_params=_cparams(("arbitrary", "arbitrary")),
        name="adaln_norm_grid",
    )(x.reshape(b, rows, GRID_W, D_MODEL), mod3, norm_g.reshape(1, D_MODEL))
    return h.reshape(b * l, D_MODEL), h_cm


def _adaln(x2, mod3, norm_g, row0, tiles_per_row, tm):
    t = x2.shape[0]
    return pl.pallas_call(
        functools.partial(_h_kernel, grid_rows=None),
        grid=(t // tm,),
        in_specs=[pl.BlockSpec((tm, D_MODEL), lambda i: (i, 0)),
                  pl.BlockSpec((1, 1, 3 * D_MODEL), lambda i: (row0 + i // tiles_per_row, 0, 0)),
                  pl.BlockSpec((1, D_MODEL), lambda i: (0, 0))],
        out_specs=pl.BlockSpec((tm, D_MODEL), lambda i: (i, 0)),
        out_shape=jax.ShapeDtypeStruct((t, D_MODEL), BF16),
        compiler_params=_cparams(("arbitrary",)),
        name="adaln_norm",
    )(x2, mod3, norm_g.reshape(1, D_MODEL))


def _proj_kernel(*refs, mode, tm, seq_len):
    halo = tm != seq_len
    cw, rb_rows = EPI_BLOCK[mode]
    if halo:
        h_ref, hp_ref, hn_ref, w_ref = refs[:4]
        params = refs[4:-1]
        lhs = jnp.concatenate([h_ref[...], hp_ref[...], hn_ref[...]], axis=0)
        start = pl.program_id(0) * tm
        keep_prev = jnp.where((start & (seq_len - 1)) != 0, 1.0, 0.0)
        keep_next = jnp.where(((start + tm) & (seq_len - 1)) != 0, 1.0, 0.0)
    else:
        h_ref, w_ref = refs[:2]
        params = refs[2:-1]
        lhs = h_ref[...]
    o_ref = refs[-1]
    pad = 8
    sub = lax.broadcasted_iota(jnp.int32, (pad, cw), 0)
    nb = rb_rows // pad
    sub3 = lax.broadcasted_iota(jnp.int32, (nb, pad, cw), 1)
    for cb in range(w_ref.shape[1] // cw):
        cols = slice(cb * cw, (cb + 1) * cw)
        p_all = _dot(lhs, w_ref[:, cols])
        if halo:
            top = jnp.where(sub == pad - 1, p_all[tm + HALO - 1:tm + HALO] * keep_prev, 0.0)
            bot = jnp.where(sub == 0, p_all[tm + HALO:tm + HALO + 1] * keep_next, 0.0)
        else:
            top = bot = jnp.zeros((pad, cw), F32)
        ext = jnp.concatenate([top, p_all[:tm], bot], axis=0)
        for rb in range(tm // rb_rows):
            r0 = pad + rb * rb_rows
            win = ext[r0 - pad:r0 + rb_rows + pad].reshape(nb + 2, pad, cw)
            cur = win[1:nb + 1]
            down = pltpu.roll(win[:nb + 1], 1, 1)
            up = pltpu.roll(win[1:], pad - 1, 1)
            prev = jnp.where(sub3 == 0, down[:nb], down[1:])
            nxt = jnp.where(sub3 == pad - 1, up[1:], up[:nb])
            if mode == "conv":
                cw_ref, cb_ref = params
                out = _silu(cw_ref[0:1, cols] * prev + cw_ref[1:2, cols] * cur
                            + cw_ref[2:3, cols] * nxt + cb_ref[:, cols])
            else:
                (mu_ref,) = params
                out = cur + mu_ref[:, cols] * (0.5 * (prev + nxt) - cur)
            o_ref[rb * rb_rows:(rb + 1) * rb_rows, cols] = out.reshape(rb_rows, cw)


EPI_BLOCK = {"conv": (256, 32), "shift": (640, 16)}


def _gates_kernel(h_ref, w_ref, g_ref, dt_ref):
    h = h_ref[...]
    for c0 in range(0, DT_OFF, ACT_COLS):
        p = _dot(h, w_ref[:, c0:c0 + ACT_COLS])
        p = _sigmoid(p) if GL_OFF <= c0 < ZB_OFF else _silu(p)
        g_ref[:, c0:c0 + ACT_COLS] = p.astype(BF16)
    dt_ref[...] = _dot(h, w_ref[:, DT_OFF:])


def _project_gates(h, w, tm=512):
    t, k = h.shape
    return pl.pallas_call(
        _gates_kernel,
        grid=(t // tm,),
        in_specs=[pl.BlockSpec((tm, k), lambda i: (i, 0)),
                  pl.BlockSpec((k, PLAIN_W), lambda i: (0, 0), pipeline_mode=pl.Buffered(1))],
        out_specs=[pl.BlockSpec((tm, DT_OFF), lambda i: (i, 0)),
                   pl.BlockSpec((tm, PLAIN_W - DT_OFF), lambda i: (i, 0))],
        out_shape=[jax.ShapeDtypeStruct((t, DT_OFF), BF16),
                   jax.ShapeDtypeStruct((t, PLAIN_W - DT_OFF), F32)],
        compiler_params=_cparams(("arbitrary",)),
        name="in_proj_gates",
    )(h, w)


def _project(h, w, mode, seq_len, params, tm=512):
    t, k = h.shape
    n = w.shape[1]
    tm = min(tm, seq_len)
    assert seq_len & (seq_len - 1) == 0 and seq_len % tm == 0 and tm % HALO == 0
    in_specs = [pl.BlockSpec((tm, k), lambda i: (i, 0))]
    args = [h]
    if tm != seq_len:
        per = tm // HALO
        in_specs += [pl.BlockSpec((HALO, k), lambda i: (jnp.maximum(i * per - 1, 0), 0)),
                     pl.BlockSpec((HALO, k), lambda i: (jnp.minimum((i + 1) * per, t // HALO - 1), 0))]
        args += [h, h]
    in_specs.append(pl.BlockSpec((k, n), lambda i: (0, 0), pipeline_mode=pl.Buffered(1)))
    args.append(w)
    for prm in params:
        in_specs.append(pl.BlockSpec(prm.shape, lambda i: (0, 0)))
        args.append(prm)
    return pl.pallas_call(
        functools.partial(_proj_kernel, mode=mode, tm=tm, seq_len=seq_len),
        grid=(t // tm,),
        in_specs=in_specs,
        out_specs=pl.BlockSpec((tm, n), lambda i: (i, 0)),
        out_shape=jax.ShapeDtypeStruct((t, n), F32),
        compiler_params=_cparams(("arbitrary",)),
        name="in_proj_" + mode,
    )(*args)


def _ssd_kernel(*refs, rev, has_init, has_prev, n_sub):
    refs = list(refs)
    xs_ref, b_ref, c_ref, cs_ref, cst_ref, dtt_ref, ddt_ref = refs[:7]
    pos = 7
    s0_ref = None
    if has_init:
        s0_ref = refs[pos]
        pos += 1
    yprev_ref = dskip_ref = None
    if has_prev:
        yprev_ref, dskip_ref = refs[pos], refs[pos + 1]
        pos += 2
    y_ref, sfin_ref, s_ref = refs[pos], refs[pos + 1], refs[pos + 2]

    q = SSM_CHUNK
    c = pl.program_id(1)

    @pl.when(c == 0)
    def _():
        if has_init:
            s_ref[...] = s0_ref[0]
        else:
            s_ref[...] = jnp.zeros_like(s_ref)

    ii = lax.broadcasted_iota(jnp.int32, (q, q), 0)
    jj = lax.broadcasted_iota(jnp.int32, (q, q), 1)
    incl = (jj >= ii) if rev else (jj <= ii)
    lane_lo = jj < SSM_HEADDIM
    row_lo = ii < SSM_HEADDIM
    last = 0 if rev else q - 1

    m_lo = lane_lo.astype(BF16)
    m_hi = jnp.logical_not(lane_lo).astype(BF16)
    subs = list(range(n_sub))[::-1] if rev else list(range(n_sub))
    for j, g in [(j, g) for j in subs for g in range(SSM_GROUPS)]:
        rows = slice(j * q, (j + 1) * q)
        cs = cs_ref[0, 0, rows, :]
        cs_t = cst_ref[0, 0, j]
        dt_t = dtt_ref[0, 0, j]
        dd_t = ddt_ref[0, 0, j]
        bm = b_ref[0, rows, g * SSM_STATE:(g + 1) * SSM_STATE].astype(BF16)
        cm = c_ref[0, rows, g * SSM_STATE:(g + 1) * SSM_STATE].astype(BF16)
        cb = _dot_nt(cm, bm)
        ps = [g * (SSM_PAIRS // SSM_GROUPS) + jp for jp in range(SSM_PAIRS // SSM_GROUPS)]
        idx = range(len(ps))
        lanes = [slice(p * LANES, (p + 1) * LANES) for p in ps]
        xs = [xs_ref[0, rows, lanes[i]] for i in idx]
        cs_col = {h: cs[:, h:h + 1] for p in ps for h in (2 * p, 2 * p + 1)}
        cs_pair = [jnp.where(lane_lo, cs_col[2 * p], cs_col[2 * p + 1]) for p in ps]
        w_pair = []
        for p in ps:
            w_heads = []
            for h in (2 * p, 2 * p + 1):
                seg = cs_col[h] - cs_t[h:h + 1, :]
                lm = jnp.where(incl, jnp.exp(seg), 0.0)
                w_heads.append((cb * lm * dt_t[h:h + 1, :]).astype(BF16))
            w_pair.append(jnp.concatenate(w_heads, axis=1))
        xs_b = [q_.astype(BF16) for q_ in xs]
        y_diag = [_dot(w_pair[i], jnp.concatenate([xs_b[i] * m_lo, xs_b[i] * m_hi], axis=0)) for i in idx]
        s_pair = [s_ref[p] for p in ps]
        y_off = [_dot_nt(cm, s_pair[i].astype(BF16)) * jnp.exp(cs_pair[i]) for i in idx]
        for i in idx:
            y = y_diag[i] + y_off[i]
            if has_prev:
                y = y + yprev_ref[0, rows, lanes[i]] + dskip_ref[:, lanes[i]] * xs[i]
            y_ref[0, rows, lanes[i]] = y
        scale = [jnp.where(row_lo, dd_t[2 * p:2 * p + 1, :], dd_t[2 * p + 1:2 * p + 2, :]) for p in ps]
        upd = [_dot((xs[i].T * scale[i]).astype(BF16), bm) for i in idx]
        for i, p in enumerate(ps):
            end_col = jnp.where(row_lo, cs_t[2 * p:2 * p + 1, last:last + 1], cs_t[2 * p + 1:2 * p + 2, last:last + 1])
            s_ref[p] = s_pair[i] * jnp.exp(end_col) + upd[i]

    @pl.when(c == pl.num_programs(1) - 1)
    def _():
        sfin_ref[0] = s_ref[...]


def _ssd_prep_kernel(raw_ref, alog_ref, dtb_ref, cs_ref, cst_ref, dtt_ref, ddt_ref, *, nchunks):
    q = SSM_CHUNK
    ii = lax.broadcasted_iota(jnp.int32, (q, q), 0)
    jj = lax.broadcasted_iota(jnp.int32, (q, q), 1)
    tri = [[(jj <= ii).astype(BF16)], [(jj >= ii).astype(BF16)]]
    last = [q - 1, 0]
    chains = [(d, ck) for d in range(2) for ck in range(nchunks)]
    rows = [slice(ck * q, (ck + 1) * q) for _, ck in chains]
    dt = [_softplus(raw_ref[0, rows[i], :] + dtb_ref[d:d + 1, :]) for i, (d, _) in enumerate(chains)]
    a = [dt[i] * (-jnp.exp(alog_ref[d:d + 1, :])) for i, (d, _) in enumerate(chains)]
    cs = [_mm(tri[d], _pieces(a[i], 3)) for i, (d, _) in enumerate(chains)]
    dd = [dt[i] * jnp.exp(cs[i][last[d]:last[d] + 1, :] - cs[i]) for i, (d, _) in enumerate(chains)]
    for i, (d, ck) in enumerate(chains):
        cs_ref[d, 0, rows[i], :] = cs[i]
        cst_ref[d, 0, ck] = cs[i].T
        dtt_ref[d, 0, ck] = dt[i].T
        ddt_ref[d, 0, ck] = dd[i].T


def _ssd_prep(p_dt3, alog2, dtb2):
    b, l, _ = p_dt3.shape
    rows = min(l, 8 * SSM_CHUNK)
    nck = rows // SSM_CHUNK
    dt_blk = 0
    t_spec = pl.BlockSpec((2, 1, nck, SSM_CHUNK, LANES), lambda i, j: (0, i, j, 0, 0))
    t_shape = jax.ShapeDtypeStruct((2, b, l // SSM_CHUNK, SSM_CHUNK, LANES), F32)
    return pl.pallas_call(
        functools.partial(_ssd_prep_kernel, nchunks=nck),
        grid=(b, l // rows),
        in_specs=[pl.BlockSpec((1, rows, LANES), lambda i, j: (i, j, dt_blk)),
                  pl.BlockSpec((2, LANES), lambda i, j: (0, 0)),
                  pl.BlockSpec((2, LANES), lambda i, j: (0, 0))],
        out_specs=[pl.BlockSpec((2, 1, rows, LANES), lambda i, j: (0, i, j, 0)), t_spec, t_spec, t_spec],
        out_shape=[jax.ShapeDtypeStruct((2, b, l, LANES), F32), t_shape, t_shape, t_shape],
        compiler_params=_cparams(("arbitrary", "arbitrary")),
        name="ssd_prep",
    )(p_dt3, alog2, dtb2)


def _ssd_scan(xbc, prep, s0, yprev, dskip, rev):
    b, l, _ = xbc.shape
    n_sub = min(SSD_SUB, l // SSM_CHUNK)
    q = SSM_CHUNK * n_sub
    nc = l // q
    cidx = (lambda c: nc - 1 - c) if rev else (lambda c: c)
    d = 1 if rev else 0
    t_spec = pl.BlockSpec((1, 1, n_sub, SSM_CHUNK, LANES), lambda i, c: (d, i, cidx(c), 0, 0))
    in_specs = [pl.BlockSpec((1, q, SSM_D_INNER), lambda i, c: (i, cidx(c), 0)),
                pl.BlockSpec((1, q, 512), lambda i, c: (i, cidx(c), SSM_D_INNER // 512)),
                pl.BlockSpec((1, q, 512), lambda i, c: (i, cidx(c), SSM_D_INNER // 512 + 1)),
                pl.BlockSpec((1, 1, q, LANES), lambda i, c: (d, i, cidx(c), 0)),
                t_spec, t_spec, t_spec]
    args = [xbc, xbc, xbc, *prep]
    if s0 is not None:
        in_specs.append(pl.BlockSpec((1, SSM_PAIRS, LANES, SSM_STATE), lambda i, c: (i, 0, 0, 0)))
        args.append(s0)
    if yprev is not None:
        in_specs.append(pl.BlockSpec((1, q, SSM_D_INNER), lambda i, c: (i, cidx(c), 0)))
        in_specs.append(pl.BlockSpec((1, SSM_D_INNER), lambda i, c: (0, 0)))
        args += [yprev, dskip]
    kern = functools.partial(_ssd_kernel, rev=rev, has_init=s0 is not None, has_prev=yprev is not None, n_sub=n_sub)
    return pl.pallas_call(
        kern,
        grid=(b, nc),
        in_specs=in_specs,
        out_specs=[pl.BlockSpec((1, q, SSM_D_INNER), lambda i, c: (i, cidx(c), 0)),
                   pl.BlockSpec((1, SSM_PAIRS, LANES, SSM_STATE), lambda i, c: (i, 0, 0, 0))],
        out_shape=[jax.ShapeDtypeStruct((b, l, SSM_D_INNER), F32),
                   jax.ShapeDtypeStruct((b, SSM_PAIRS, LANES, SSM_STATE), F32)],
        scratch_shapes=[pltpu.VMEM((SSM_PAIRS, LANES, SSM_STATE), F32)],
        compiler_params=_cparams(("arbitrary", "arbitrary")),
        name="ssd_bwd" if rev else "ssd_fwd",
    )(*args)


CUM_PIECES = 2


def _pieces(x, n):
    out = []
    for i in range(n):
        h = x.astype(BF16)
        out.append(h)
        if i + 1 < n:
            x = x - h.astype(F32)
    return out


def _mm(a_pieces, b_pieces, nt=False):
    dot = _dot_nt if nt else _dot
    depth = max(len(a_pieces), len(b_pieces))
    acc = None
    for i in reversed(range(len(a_pieces))):
        for j in reversed(range(len(b_pieces))):
            if i + j < depth:
                t = dot(a_pieces[i], b_pieces[j])
                acc = t if acc is None else acc + t
    return acc


def _block_diag(y, m_lo, m_hi):
    return jnp.concatenate([y * m_lo, y * m_hi], axis=0)


def _wkv_kernel(*refs, rev, has_init, final, grid_rows):
    refs = list(refs)
    (rw_ref, w0_ref, w2_ref, a0_ref, a2_ref, kk_ref, ka_ref) = refs[:7]
    pos = 7
    s0_ref = None
    if has_init:
        s0_ref = refs[pos]
        pos += 1
    of_ref = rk_ref = lnw_ref = lnb_ref = None
    if final:
        of_ref, rk_ref, lnw_ref, lnb_ref = refs[pos:pos + 4]
        pos += 4
    o_ref, sfin_ref, s_ref = refs[pos], refs[pos + 1], refs[pos + 2]

    n = WKV_CHUNK
    c = pl.program_id(1)

    @pl.when(c == 0)
    def _():
        if has_init:
            s_ref[...] = s0_ref[0]
        else:
            s_ref[...] = jnp.zeros_like(s_ref)

    xl = rw_ref[0, :, 3 * WKV_WIDTH:3 * WKV_WIDTH + LANES]
    a_full = _sigmoid(a0_ref[...] + _dot(xl.astype(BF16), a2_ref[...]))
    xw = w0_ref[...] + _dot(jnp.tanh(xl).astype(BF16), w2_ref[...])
    lw_full = -math.exp(-0.5) * _sigmoid(xw)

    ti = lax.broadcasted_iota(jnp.int32, (n, LANES), 0)
    li = lax.broadcasted_iota(jnp.int32, (n, LANES), 1)
    si = li & (WKV_HEADSIZE - 1)
    strict = (si > ti) if rev else (si < ti)
    incl = (si >= ti) if rev else (si <= ti)
    eye = (si == ti).astype(BF16)
    m_lo = (li < WKV_HEADSIZE).astype(BF16)
    m_hi = (li >= WKV_HEADSIZE).astype(BF16)
    tq = lax.broadcasted_iota(jnp.int32, (n, n), 0)
    sq = lax.broadcasted_iota(jnp.int32, (n, n), 1)
    tri = jnp.concatenate([((sq >= tq) if rev else (sq <= tq)).astype(BF16)] * CUM_PIECES, axis=1)
    r2 = lax.broadcasted_iota(jnp.int32, (LANES, LANES), 0)
    c2 = lax.broadcasted_iota(jnp.int32, (LANES, LANES), 1)
    same_head = (r2 < WKV_HEADSIZE) == (c2 < WKV_HEADSIZE)
    ones_bd = same_head.astype(BF16)
    last = 0 if rev else n - 1

    def level_mask(m):
        same = (ti >> (m.bit_length())) == (si >> (m.bit_length()))
        t_hi = (ti & m) != 0
        s_hi = (si & m) != 0
        if rev:
            return same & jnp.logical_not(t_hi) & s_hi
        return same & t_hi & jnp.logical_not(s_hi)

    levels = []
    m = 2
    while m < n:
        levels.append(level_mask(m).astype(BF16))
        m *= 2
    level1 = level_mask(1).astype(BF16)

    def bd(y):
        return _block_diag(y.astype(BF16), m_lo, m_hi)

    def seg_sum(x):
        return _dot(x.astype(BF16), ones_bd)

    def lane_blk(p, base=0):
        return slice(base + p * LANES, base + (p + 1) * LANES)

    def rows(j):
        return slice(j * n, (j + 1) * n)

    subs = list(range(WKV_SUB))[::-1] if rev else list(range(WKV_SUB))
    chains = [(j, p) for j in subs for p in range(N_PAIRS)]
    idx = range(len(chains))
    r = [rw_ref[0, rows(j), lane_blk(p)] for j, p in chains]
    k = [rw_ref[0, rows(j), lane_blk(p, WKV_WIDTH)] for j, p in chains]
    v = [rw_ref[0, rows(j), lane_blk(p, 2 * WKV_WIDTH)] for j, p in chains]
    a = [a_full[rows(j), lane_blk(p)] for j, p in chains]
    lw = [lw_full[rows(j), lane_blk(p)] for j, p in chains]
    kkr = [k[i] * kk_ref[p] for i, (j, p) in enumerate(chains)]
    ss = [seg_sum(q * q) for q in kkr]
    kk = [kkr[i] / jnp.maximum(jnp.sqrt(ss[i]), 1e-12) for i in idx]
    kmod = [k[i] * (1.0 + (a[i] - 1.0) * ka_ref[p]) for i, (j, p) in enumerate(chains)]
    kka = [kk[i] * a[i] for i in idx]

    lg = [_dot(tri, jnp.concatenate(_pieces(q, CUM_PIECES), axis=0)) for q in lw]
    g_inv = [jnp.exp(-q) for q in lg]
    rh = [r[i] * jnp.exp(lg[i]) for i in idx]
    kh = [kmod[i] * g_inv[i] for i in idx]
    ah = [kka[i] * g_inv[i] for i in idx]
    bh = [-kk[i] * jnp.exp(lg[i] - lw[i]) for i in idx]
    g_end = [jnp.exp(q[last:last + 1, :]) for q in lg]

    lhs = [jnp.concatenate([bh[i], rh[i]], axis=0).astype(BF16) for i in idx]
    rhs = [jnp.concatenate([bd(ah[i]), bd(kh[i])], axis=0) for i in idx]
    aak = [_dot_nt(lhs[i], rhs[i]) for i in idx]
    a_ab = [jnp.where(strict, q[:n, :LANES], 0.0).astype(BF16) for q in aak]
    a_ra = [jnp.where(incl, q[n:, :LANES], 0.0).astype(BF16) for q in aak]
    a_bkrk = [jnp.concatenate([jnp.where(strict, q[:n, LANES:], 0.0), jnp.where(incl, q[n:, LANES:], 0.0)],
                              axis=0).astype(BF16) for q in aak]

    x = [eye + q * level1 for q in a_ab]
    for lm in levels:
        pm = [_dot(x[i], bd(a_ab[i] * lm)).astype(BF16) for i in idx]
        x = [x[i] + _dot(pm[i], bd(x[i])).astype(BF16) for i in idx]
    gv = [_dot(a_bkrk[i], bd(v[i])) for i in idx]
    ake = [(jnp.concatenate([ah[i], kh[i]], axis=0) * g_end[i]).astype(BF16) for i in idx]

    o = [None] * len(chains)
    for jj in range(WKV_SUB):
        ids = list(range(jj * N_PAIRS, (jj + 1) * N_PAIRS))
        s_bd = {i: s_ref[chains[i][1]] for i in ids}
        x0r = {i: _dot_nt(lhs[i], s_bd[i].astype(BF16)) for i in ids}
        u = {i: _dot(x[i], bd(x0r[i][:n] + gv[i][:n])) for i in ids}
        for i in ids:
            o[i] = x0r[i][n:] + gv[i][n:] + _dot(a_ra[i], bd(u[i]))
        uvt = {i: jnp.concatenate([u[i], v[i]], axis=0).T.astype(BF16) for i in ids}
        upd = {i: _dot(uvt[i], ake[i]) for i in ids}
        for i in ids:
            s_ref[chains[i][1]] = s_bd[i] * g_end[i] + jnp.where(same_head, upd[i], 0.0)

    if final:
        o = [o[i] + of_ref[0, p, rows(j), :] for i, (j, p) in enumerate(chains)]
        mu = [seg_sum(q) * (1.0 / WKV_HEADSIZE) for q in o]
        d = [o[i] - mu[i] for i in idx]
        var = [seg_sum(q * q) * (1.0 / WKV_HEADSIZE) for q in d]
        bonus = [seg_sum(r[i] * kmod[i] * rk_ref[p]) * v[i] for i, (j, p) in enumerate(chains)]
        o = [d[i] * lax.rsqrt(var[i] + WKV_GN_EPS) * lnw_ref[p] + lnb_ref[p] + bonus[i]
             for i, (j, p) in enumerate(chains)]
    if grid_rows is None:
        for i, (j, p) in enumerate(chains):
            o_ref[0, p, rows(j), :] = o[i]
    else:
        step = (pl.num_programs(1) - 1 - c) if rev else c
        cols_per_sub = n // grid_rows
        for i, (j, p) in enumerate(chains):
            for wl in range(cols_per_sub):
                w = (step * WKV_SUB + j) * cols_per_sub + wl
                o_ref[0, p, pl.ds(w, grid_rows, stride=GRID_W), :] = o[i][wl * grid_rows:(wl + 1) * grid_rows]

    @pl.when(c == pl.num_programs(1) - 1)
    def _():
        sfin_ref[0] = s_ref[...]


def _pairs(vec):
    return vec.reshape(N_PAIRS, 1, LANES)


def _wkv_scan(rw_s, w0, w2p, a0, a2p, k_k, k_a, s0, o_f, r_k, ln_w, ln_b, rev, to_row_major=False):
    b, l, w = rw_s.shape
    n = WKV_CHUNK * WKV_SUB
    nc = l // n
    cidx = (lambda c: nc - 1 - c) if rev else (lambda c: c)
    final = o_f is not None
    vec_spec = pl.BlockSpec((N_PAIRS, 1, LANES), lambda i, c: (0, 0, 0))
    st_spec = pl.BlockSpec((1, N_PAIRS, LANES, LANES), lambda i, c: (i, 0, 0, 0))
    in_specs = [pl.BlockSpec((1, n, w), lambda i, c: (i, cidx(c), 0)),
                pl.BlockSpec((1, WKV_WIDTH), lambda i, c: (0, 0)),
                pl.BlockSpec((LANES, WKV_WIDTH), lambda i, c: (0, 0)),
                pl.BlockSpec((1, WKV_WIDTH), lambda i, c: (0, 0)),
                pl.BlockSpec((LANES, WKV_WIDTH), lambda i, c: (0, 0)),
                vec_spec, vec_spec]
    args = [rw_s, w0.reshape(1, WKV_WIDTH), w2p, a0.reshape(1, WKV_WIDTH), a2p, _pairs(k_k), _pairs(k_a)]
    if s0 is not None:
        in_specs.append(st_spec)
        args.append(s0)
    if final:
        in_specs += [pl.BlockSpec((1, N_PAIRS, n, LANES), lambda i, c: (i, 0, cidx(c), 0)),
                     vec_spec, vec_spec, vec_spec]
        args += [o_f, _pairs(r_k), _pairs(ln_w), _pairs(ln_b)]
    o_spec = pl.BlockSpec((1, N_PAIRS, n, LANES), lambda i, c: (i, 0, cidx(c), 0))
    grid_rows = None
    if to_row_major:
        o_spec = pl.BlockSpec((1, N_PAIRS, l, LANES), lambda i, c: (i, 0, 0, 0))
        grid_rows = l // GRID_W
        assert WKV_CHUNK % grid_rows == 0
    kern = functools.partial(_wkv_kernel, rev=rev, has_init=s0 is not None, final=final, grid_rows=grid_rows)
    return pl.pallas_call(
        kern,
        grid=(b, nc),
        in_specs=in_specs,
        out_specs=[o_spec, st_spec],
        out_shape=[jax.ShapeDtypeStruct((b, N_PAIRS, l, LANES), F32),
                   jax.ShapeDtypeStruct((b, N_PAIRS, LANES, LANES), F32)],
        scratch_shapes=[pltpu.VMEM((N_PAIRS, LANES, LANES), F32)],
        compiler_params=_cparams(("arbitrary", "arbitrary")),
        name="wkv_bwd" if rev else "wkv_fwd",
    )(*args)


def _final_kernel(y_ref, za_ref, o_ref, zb_ref, ga_ref, gb_ref, x_ref, mod_ref,
                  ng_ref, pa_ref, pb_ref, wo_ref, fg_ref, out_ref):
    y = y_ref[...] * za_ref[...].astype(F32)
    y = y * lax.rsqrt(jnp.mean(y * y, axis=-1, keepdims=True) + NORM_EPS) * ng_ref[...]
    u_a = _dot(y.astype(BF16), pa_ref[...])
    o = jnp.concatenate([jnp.concatenate([o_ref[s, p] for p in range(N_PAIRS)], axis=1)
                         for s in range(o_ref.shape[0])], axis=0)
    u_b = _dot((o * zb_ref[...].astype(F32)).astype(BF16), pb_ref[...])
    m = ga_ref[...].astype(F32) * u_a + gb_ref[...].astype(F32) * u_b
    out = _dot(m.astype(BF16), wo_ref[...])
    gate = mod_ref[0][:, 2 * D_MODEL:]
    xo = x_ref[...] + gate * out
    out_ref[...] = xo * lax.rsqrt(jnp.mean(xo * xo, axis=-1, keepdims=True) + NORM_EPS) * fg_ref[...]


def _final(y2, p_plain, o4, x2, mod3, ssm_norm_g, p_a, p_b, w_out, final_g, row0, tiles_per_row, tm):
    t = x2.shape[0]
    w1 = D_MODEL
    l = o4.shape[2]
    o_rows = min(tm, l)
    tiles_per_seq = l // o_rows

    def resident(shape):
        return pl.BlockSpec(shape, lambda i: (0, 0), pipeline_mode=pl.Buffered(1))

    return pl.pallas_call(
        _final_kernel,
        grid=(t // tm,),
        in_specs=[pl.BlockSpec((tm, SSM_D_INNER), lambda i: (i, 0)),
                  pl.BlockSpec((tm, SSM_D_INNER), lambda i: (i, 0)),
                  pl.BlockSpec((tm // o_rows, N_PAIRS, o_rows, LANES),
                               lambda i: (i // tiles_per_seq, 0, i % tiles_per_seq, 0)),
                  pl.BlockSpec((tm, w1), lambda i: (i, ZB_OFF // w1)),
                  pl.BlockSpec((tm, w1), lambda i: (i, GL_OFF // w1)),
                  pl.BlockSpec((tm, w1), lambda i: (i, GL_OFF // w1 + 1)),
                  pl.BlockSpec((tm, w1), lambda i: (i, 0)),
                  pl.BlockSpec((1, 1, 3 * D_MODEL), lambda i: (row0 + i // tiles_per_row, 0, 0)),
                  resident((1, SSM_D_INNER)),
                  resident((SSM_D_INNER, D_MODEL)),
                  resident((WKV_WIDTH, D_MODEL)),
                  resident((D_MODEL, D_MODEL)),
                  resident((1, D_MODEL))],
        out_specs=pl.BlockSpec((tm, D_MODEL), lambda i: (i, 0)),
        out_shape=jax.ShapeDtypeStruct((t, D_MODEL), F32),
        compiler_params=_cparams(("arbitrary",), MERGE_VMEM_LIMIT),
        name="merge_out",
    )(y2, p_plain, o4, p_plain, p_plain, p_plain, x2, mod3,
      ssm_norm_g.reshape(1, SSM_D_INNER), p_a, p_b, w_out, final_g.reshape(1, D_MODEL))


def _wkv_state_to_pairs(s):
    b = s.shape[0]
    s = s.reshape(b, N_PAIRS, 2, WKV_HEADSIZE, WKV_HEADSIZE)
    z = jnp.zeros_like(s[:, :, 0])
    top = jnp.concatenate([s[:, :, 0], z], axis=-1)
    bot = jnp.concatenate([z, s[:, :, 1]], axis=-1)
    return jnp.concatenate([top, bot], axis=-2)


def _wkv_state_from_pairs(s):
    b = s.shape[0]
    h = WKV_HEADSIZE
    return jnp.stack([s[:, :, :h, :h], s[:, :, h:, h:]], axis=2).reshape(b, WKV_HEADS, h, h)


def _group(x, mod3, row0, grid, states, wts):
    b, l, _ = x.shape
    t = b * l
    x2 = x.reshape(t, D_MODEL)
    if grid:
        h, h_rw = _adaln_grid(x, mod3, wts["norm_g"], row0)
    else:
        h = h_rw = _adaln(x2, mod3, wts["norm_g"], row0, t // ADALN_TM, ADALN_TM)
    p_plain, p_dt = _project_gates(h, wts["w_plain"])
    p_dt3 = p_dt.reshape(b, l, PLAIN_W - DT_OFF)

    xbc = _project(h, wts["w_xbc"], "conv", l,
                   (wts["conv_w"], wts["conv_b"].reshape(1, XBC_W))).reshape(b, l, XBC_W)
    s_f = s_b = None
    if states is not None:
        s_f = states[0].reshape(b, SSM_PAIRS, LANES, SSM_STATE)
        s_b = states[1].reshape(b, SSM_PAIRS, LANES, SSM_STATE)
    prep = _ssd_prep(p_dt3, wts["alog"], wts["dtb"])
    y_f, fs_f = _ssd_scan(xbc, prep, s_f, None, None, rev=False)
    y, fs_b = _ssd_scan(xbc, prep, s_b, y_f, wts["dskip"], rev=True)

    rw_s = _project(h_rw, wts["w_rw"], "shift", l,
                    (wts["shift_mu"].reshape(1, RWKV_SCAN_W),)).reshape(b, l, RWKV_SCAN_W)
    w_f = w_b = None
    if states is not None:
        w_f = _wkv_state_to_pairs(states[2])
        w_b = _wkv_state_to_pairs(states[3])
    o_f, fw_f = _wkv_scan(rw_s, wts["w0"][0], wts["w2p"][0], wts["a0"], wts["a2p"], wts["k_k"], wts["k_a"],
                          w_f, None, None, None, None, rev=False)
    o, fw_b = _wkv_scan(rw_s, wts["w0"][1], wts["w2p"][1], wts["a0"], wts["a2p"], wts["k_k"], wts["k_a"],
                        w_b, o_f, wts["r_k"], wts["ln_w"], wts["ln_b"], rev=True, to_row_major=grid)

    out = _final(y.reshape(t, SSM_D_INNER), p_plain, o, x2, mod3, wts["ssm_norm_g"],
                 wts["p_a"], wts["p_b"], wts["w_out"], wts["final_g"], row0,
                 (l if grid else t) // FINAL_TM, FINAL_TM)
    finals = (fs_f.reshape(b, SSM_HEADS, SSM_HEADDIM, SSM_STATE), fs_b.reshape(b, SSM_HEADS, SSM_HEADDIM, SSM_STATE),
              _wkv_state_from_pairs(fw_f), _wkv_state_from_pairs(fw_b))
    return out.reshape(b, l, D_MODEL), finals


def kernel(x_prompt, x_sample, state_ssm_fwd, state_ssm_bwd, state_wkv_fwd, state_wkv_bwd, c, c_ctx, w_mod, b_mod, norm_g, w_in, conv_w, conv_b, a_log, dt_bias, d_skip, ssm_norm_g, p_a, shift_mu, w0, w2, a0, a2, k_k, k_a, r_k, ln_w, ln_b, p_b, w_out, final_g):
    depth = w_mod.shape[0]
    assert depth == 1, "single-layer stack only"
    l0 = 0
    w_in0 = w_in[l0].astype(BF16)
    zpad = jnp.zeros((WKV_RANK, WKV_WIDTH), F32)
    w_plain = jnp.concatenate([w_in0[:, :ZA_END], w_in0[:, ZB_END:], w_in0[:, RW_END:ZB_END], w_in0[:, XBC_END:DT_END],
                               jnp.zeros((D_MODEL, PLAIN_W - DT_OFF - SSM_HEADS), BF16)], axis=1)
    wts = {
        "norm_g": norm_g[l0],
        "w_plain": w_plain,
        "w_xbc": w_in0[:, ZA_END:XBC_END],
        "w_rw": w_in0[:, DT_END:RW_END],
        "conv_w": conv_w[l0], "conv_b": conv_b[l0],
        "alog": jnp.pad(a_log[l0], ((0, 0), (0, LANES - SSM_HEADS))),
        "dtb": jnp.pad(dt_bias[l0], ((0, 0), (0, LANES - SSM_HEADS))),
        "dskip": jnp.repeat(d_skip[l0], SSM_HEADDIM).reshape(1, SSM_D_INNER),
        "ssm_norm_g": ssm_norm_g[l0],
        "p_a": p_a[l0].astype(BF16), "p_b": p_b[l0].astype(BF16), "w_out": w_out[l0].astype(BF16),
        "shift_mu": shift_mu[l0],
        "w0": w0[l0],
        "w2p": [jnp.concatenate([w2[l0, d], zpad], axis=0).astype(BF16) for d in range(2)],
        "a0": a0[l0],
        "a2p": jnp.concatenate([zpad, a2[l0]], axis=0).astype(BF16),
        "k_k": k_k[l0], "k_a": k_a[l0], "r_k": r_k[l0], "ln_w": ln_w[l0], "ln_b": ln_b[l0],
        "final_g": final_g,
    }
    nb = c.shape[0]
    cond8 = jnp.concatenate([c_ctx[None, :], c, jnp.zeros((8 - 1 - nb, D_MODEL), F32)], axis=0)
    mod3 = _modulation(cond8, w_mod[l0], b_mod[l0]).reshape(8, 1, 3 * D_MODEL)

    y_prompt, (sf, sb, wf, wb) = _group(x_prompt, mod3, 0, False, None, wts)
    lat_states = (state_ssm_fwd[:, l0], state_ssm_bwd[:, l0], state_wkv_fwd[:, l0], state_wkv_bwd[:, l0])
    y_sample, _ = _group(x_sample, mod3, 1, True, lat_states, wts)
    return (y_prompt, y_sample, sf[:, None], sb[:, None], wf[:, None], wb[:, None])
```

```python
import functools
import math

import jax
import jax.numpy as jnp
from jax import lax
from jax.experimental import pallas as pl
from jax.experimental.pallas import tpu as pltpu

F32 = jnp.float32
BF16 = jnp.bfloat16
HI = lax.Precision.HIGHEST

D_MODEL = 1024
GRID_W = 64
NORM_EPS = 1e-6
SSM_D_INNER = 2048
SSM_HEADDIM = 64
SSM_HEADS = 32
SSM_GROUPS = 4
SSM_STATE = 128
SSM_CHUNK = 128
WKV_WIDTH = 1024
WKV_HEADSIZE = 64
WKV_HEADS = 16
WKV_RANK = 64
WKV_GN_EPS = 64e-5
WKV_CHUNK = 64
XBC_W = SSM_D_INNER + 2 * SSM_GROUPS * SSM_STATE
RWKV_SCAN_W = 3 * WKV_WIDTH + 2 * WKV_RANK
ZA_END = SSM_D_INNER
XBC_END = ZA_END + XBC_W
DT_END = XBC_END + SSM_HEADS
RW_END = DT_END + RWKV_SCAN_W
ZB_END = RW_END + WKV_WIDTH
IN_W = ZB_END + 2 * D_MODEL

LANES = 128
N_PAIRS = WKV_HEADS // 2
SSM_PAIRS = SSM_HEADS // 2
WKV_SUB = 4
SSD_SUB = 4
FINAL_TM = 512
ADALN_TM = 1024
CM_COLS = 16
HALO = 16
GL_OFF = SSM_D_INNER
ZB_OFF = GL_OFF + 2 * D_MODEL
DT_OFF = ZB_OFF + WKV_WIDTH
ACT_COLS = 2 * LANES
PLAIN_W = 21 * ACT_COLS
VMEM_LIMIT = 48 * 1024 * 1024
MERGE_VMEM_LIMIT = 56 * 1024 * 1024


def _cparams(sem, vmem_limit=VMEM_LIMIT):
    return pltpu.CompilerParams(dimension_semantics=sem, vmem_limit_bytes=vmem_limit)


def _sigmoid(x):
    return 1.0 / (1.0 + jnp.exp2(x * (-1.0 / math.log(2.0))))


def _silu(x):
    return x * _sigmoid(x)


def _softplus(x):
    return jnp.maximum(x, 0.0) + jnp.log(1.0 + jnp.exp(-jnp.abs(x)))


def _dot(a, b, precision=None):
    return jnp.dot(a, b, preferred_element_type=F32, precision=precision)


def _dot_nt(a, b, precision=None):
    return lax.dot_general(a, b, (((1,), (1,)), ((), ())), preferred_element_type=F32, precision=precision)


def _mod_kernel(c_ref, w_ref, b_ref, o_ref):
    c = c_ref[...]
    o_ref[...] = _dot(_silu(c), w_ref[...], HI) + b_ref[...]


def _modulation(cond8, w_mod, b_mod):
    n = w_mod.shape[1]
    tn = 1024
    return pl.pallas_call(
        _mod_kernel,
        grid=(n // tn,),
        in_specs=[pl.BlockSpec((8, D_MODEL), lambda j: (0, 0)),
                  pl.BlockSpec((D_MODEL, tn), lambda j: (0, j)),
                  pl.BlockSpec((1, tn), lambda j: (0, j))],
        out_specs=pl.BlockSpec((8, tn), lambda j: (0, j)),
        out_shape=jax.ShapeDtypeStruct((8, n), F32),
        compiler_params=_cparams(("arbitrary",)),
        name="modulation",
    )(cond8, w_mod, b_mod.reshape(1, n))


def _h_kernel(x_ref, mod_ref, g_ref, h_ref, *maybe_cm_ref, grid_rows):
    if grid_rows is None:
        x = x_ref[...]
    else:
        x = x_ref[0].reshape(grid_rows * CM_COLS, D_MODEL)
    y = x * lax.rsqrt(jnp.mean(x * x, axis=-1, keepdims=True) + NORM_EPS) * g_ref[...]
    m = mod_ref[0]
    shift = m[:, :D_MODEL]
    scale = m[:, D_MODEL:2 * D_MODEL]
    h = (y * (1.0 + scale) + shift).astype(BF16)
    if grid_rows is None:
        h_ref[...] = h
        return
    h_ref[0] = h.reshape(grid_rows, CM_COLS, D_MODEL)
    (cm_ref,) = maybe_cm_ref
    n = grid_rows * CM_COLS
    dst = lax.broadcasted_iota(jnp.int32, (n, n), 0)
    src = lax.broadcasted_iota(jnp.int32, (n, n), 1)
    perm = (src == (dst % grid_rows) * CM_COLS + dst // grid_rows).astype(BF16)
    cm_ref[...] = _dot(perm, h).astype(BF16)


def _adaln_grid(x, mod3, norm_g, row0):
    b, l, _ = x.shape
    rows = l // GRID_W
    tile = rows * CM_COLS
    x_spec = pl.BlockSpec((1, rows, CM_COLS, D_MODEL), lambda i, j: (i, 0, j, 0))
    h, h_cm = pl.pallas_call(
        functools.partial(_h_kernel, grid_rows=rows),
        grid=(b, GRID_W // CM_COLS),
        in_specs=[x_spec,
                  pl.BlockSpec((1, 1, 3 * D_MODEL), lambda i, j: (row0 + i, 0, 0)),
                  pl.BlockSpec((1, D_MODEL), lambda i, j: (0, 0))],
        out_specs=[x_spec, pl.BlockSpec((tile, D_MODEL), lambda i, j: (i * (GRID_W // CM_COLS) + j, 0))],
        out_shape=[jax.ShapeDtypeStruct((b, rows, GRID_W, D_MODEL), BF16),
                   jax.ShapeDtypeStruct((b * l, D_MODEL), BF16)],
        compiler_params=_cparams(("arbitrary", "arbitrary")),
        name="adaln_norm_grid",
    )(x.reshape(b, rows, GRID_W, D_MODEL), mod3, norm_g.reshape(1, D_MODEL))
    return h.reshape(b * l, D_MODEL), h_cm


def _adaln(x2, mod3, norm_g, row0, tiles_per_row, tm):
    t = x2.shape[0]
    return pl.pallas_call(
        functools.partial(_h_kernel, grid_rows=None),
        grid=(t // tm,),
        in_specs=[pl.BlockSpec((tm, D_MODEL), lambda i: (i, 0)),
                  pl.BlockSpec((1, 1, 3 * D_MODEL), lambda i: (row0 + i // tiles_per_row, 0, 0)),
                  pl.BlockSpec((1, D_MODEL), lambda i: (0, 0))],
        out_specs=pl.BlockSpec((tm, D_MODEL), lambda i: (i, 0)),
        out_shape=jax.ShapeDtypeStruct((t, D_MODEL), BF16),
        compiler_params=_cparams(("arbitrary",)),
        name="adaln_norm",
    )(x2, mod3, norm_g.reshape(1, D_MODEL))


def _proj_kernel(*refs, mode, tm, seq_len):
    halo = tm != seq_len
    cw, rb_rows = EPI_BLOCK[mode]
    if halo:
        h_ref, hp_ref, hn_ref, w_ref = refs[:4]
        params = refs[4:-1]
        lhs = jnp.concatenate([h_ref[...], hp_ref[...], hn_ref[...]], axis=0)
        start = pl.program_id(0) * tm
        keep_prev = jnp.where((start & (seq_len - 1)) != 0, 1.0, 0.0)
        keep_next = jnp.where(((start + tm) & (seq_len - 1)) != 0, 1.0, 0.0)
    else:
        h_ref, w_ref = refs[:2]
        params = refs[2:-1]
        lhs = h_ref[...]
    o_ref = refs[-1]
    pad = 8
    sub = lax.broadcasted_iota(jnp.int32, (pad, cw), 0)
    nb = rb_rows // pad
    sub3 = lax.broadcasted_iota(jnp.int32, (nb, pad, cw), 1)
    for cb in range(w_ref.shape[1] // cw):
        cols = slice(cb * cw, (cb + 1) * cw)
        p_all = _dot(lhs, w_ref[:, cols])
        if halo:
            top = jnp.where(sub == pad - 1, p_all[tm + HALO - 1:tm + HALO] * keep_prev, 0.0)
            bot = jnp.where(sub == 0, p_all[tm + HALO:tm + HALO + 1] * keep_next, 0.0)
        else:
            top = bot = jnp.zeros((pad, cw), F32)
        ext = jnp.concatenate([top, p_all[:tm], bot], axis=0)
        for rb in range(tm // rb_rows):
            r0 = pad + rb * rb_rows
            win = ext[r0 - pad:r0 + rb_rows + pad].reshape(nb + 2, pad, cw)
            cur = win[1:nb + 1]
            down = pltpu.roll(win[:nb + 1], 1, 1)
            up = pltpu.roll(win[1:], pad - 1, 1)
            prev = jnp.where(sub3 == 0, down[:nb], down[1:])
            nxt = jnp.where(sub3 == pad - 1, up[1:], up[:nb])
            if mode == "conv":
                cw_ref, cb_ref = params
                out = _silu(cw_ref[0:1, cols] * prev + cw_ref[1:2, cols] * cur
                            + cw_ref[2:3, cols] * nxt + cb_ref[:, cols])
            else:
                (mu_ref,) = params
                out = cur + mu_ref[:, cols] * (0.5 * (prev + nxt) - cur)
            o_ref[rb * rb_rows:(rb + 1) * rb_rows, cols] = out.reshape(rb_rows, cw)


EPI_BLOCK = {"conv": (256, 32), "shift": (640, 16)}


def _gates_kernel(h_ref, w_ref, g_ref, dt_ref):
    h = h_ref[...]
    for c0 in range(0, DT_OFF, ACT_COLS):
        p = _dot(h, w_ref[:, c0:c0 + ACT_COLS])
        p = _sigmoid(p) if GL_OFF <= c0 < ZB_OFF else _silu(p)
        g_ref[:, c0:c0 + ACT_COLS] = p.astype(BF16)
    dt_ref[...] = _dot(h, w_ref[:, DT_OFF:])


def _project_gates(h, w, tm=512):
    t, k = h.shape
    return pl.pallas_call(
        _gates_kernel,
        grid=(t // tm,),
        in_specs=[pl.BlockSpec((tm, k), lambda i: (i, 0)),
                  pl.BlockSpec((k, PLAIN_W), lambda i: (0, 0), pipeline_mode=pl.Buffered(1))],
        out_specs=[pl.BlockSpec((tm, DT_OFF), lambda i: (i, 0)),
                   pl.BlockSpec((tm, PLAIN_W - DT_OFF), lambda i: (i, 0))],
        out_shape=[jax.ShapeDtypeStruct((t, DT_OFF), BF16),
                   jax.ShapeDtypeStruct((t, PLAIN_W - DT_OFF), F32)],
        compiler_params=_cparams(("arbitrary",)),
        name="in_proj_gates",
    )(h, w)


def _project(h, w, mode, seq_len, params, tm=512):
    t, k = h.shape
    n = w.shape[1]
    tm = min(tm, seq_len)
    assert seq_len & (seq_len - 1) == 0 and seq_len % tm == 0 and tm % HALO == 0
    in_specs = [pl.BlockSpec((tm, k), lambda i: (i, 0))]
    args = [h]
    if tm != seq_len:
        per = tm // HALO
        in_specs += [pl.BlockSpec((HALO, k), lambda i: (jnp.maximum(i * per - 1, 0), 0)),
                     pl.BlockSpec((HALO, k), lambda i: (jnp.minimum((i + 1) * per, t // HALO - 1), 0))]
        args += [h, h]
    in_specs.append(pl.BlockSpec((k, n), lambda i: (0, 0), pipeline_mode=pl.Buffered(1)))
    args.append(w)
    for prm in params:
        in_specs.append(pl.BlockSpec(prm.shape, lambda i: (0, 0)))
        args.append(prm)
    return pl.pallas_call(
        functools.partial(_proj_kernel, mode=mode, tm=tm, seq_len=seq_len),
        grid=(t // tm,),
        in_specs=in_specs,
        out_specs=pl.BlockSpec((tm, n), lambda i: (i, 0)),
        out_shape=jax.ShapeDtypeStruct((t, n), F32),
        compiler_params=_cparams(("arbitrary",)),
        name="in_proj_" + mode,
    )(*args)


def _ssd_kernel(*refs, rev, has_init, has_prev, n_sub):
    refs = list(refs)
    xs_ref, b_ref, c_ref, cs_ref, cst_ref, dtt_ref, ddt_ref = refs[:7]
    pos = 7
    s0_ref = None
    if has_init:
        s0_ref = refs[pos]
        pos += 1
    yprev_ref = dskip_ref = None
    if has_prev:
        yprev_ref, dskip_ref = refs[pos], refs[pos + 1]
        pos += 2
    y_ref, sfin_ref, s_ref = refs[pos], refs[pos + 1], refs[pos + 2]

    q = SSM_CHUNK
    c = pl.program_id(1)

    @pl.when(c == 0)
    def _():
        if has_init:
            s_ref[...] = s0_ref[0]
        else:
            s_ref[...] = jnp.zeros_like(s_ref)

    ii = lax.broadcasted_iota(jnp.int32, (q, q), 0)
    jj = lax.broadcasted_iota(jnp.int32, (q, q), 1)
    incl = (jj >= ii) if rev else (jj <= ii)
    lane_lo = jj < SSM_HEADDIM
    row_lo = ii < SSM_HEADDIM
    last = 0 if rev else q - 1

    m_lo = lane_lo.astype(BF16)
    m_hi = jnp.logical_not(lane_lo).astype(BF16)
    subs = list(range(n_sub))[::-1] if rev else list(range(n_sub))
    for j, g in [(j, g) for j in subs for g in range(SSM_GROUPS)]:
        rows = slice(j * q, (j + 1) * q)
        cs = cs_ref[0, 0, rows, :]
        cs_t = cst_ref[0, 0, j]
        dt_t = dtt_ref[0, 0, j]
        dd_t = ddt_ref[0, 0, j]
        bm = b_ref[0, rows, g * SSM_STATE:(g + 1) * SSM_STATE].astype(BF16)
        cm = c_ref[0, rows, g * SSM_STATE:(g + 1) * SSM_STATE].astype(BF16)
        cb = _dot_nt(cm, bm)
        ps = [g * (SSM_PAIRS // SSM_GROUPS) + jp for jp in range(SSM_PAIRS // SSM_GROUPS)]
        idx = range(len(ps))
        lanes = [slice(p * LANES, (p + 1) * LANES) for p in ps]
        xs = [xs_ref[0, rows, lanes[i]] for i in idx]
        cs_col = {h: cs[:, h:h + 1] for p in ps for h in (2 * p, 2 * p + 1)}
        cs_pair = [jnp.where(lane_lo, cs_col[2 * p], cs_col[2 * p + 1]) for p in ps]
        w_pair = []
        for p in ps:
            w_heads = []
            for h in (2 * p, 2 * p + 1):
                seg = cs_col[h] - cs_t[h:h + 1, :]
                lm = jnp.where(incl, jnp.exp(seg), 0.0)
                w_heads.append((cb * lm * dt_t[h:h + 1, :]).astype(BF16))
            w_pair.append(jnp.concatenate(w_heads, axis=1))
        xs_b = [q_.astype(BF16) for q_ in xs]
        y_diag = [_dot(w_pair[i], jnp.concatenate([xs_b[i] * m_lo, xs_b[i] * m_hi], axis=0)) for i in idx]
        s_pair = [s_ref[p] for p in ps]
        y_off = [_dot_nt(cm, s_pair[i].astype(BF16)) * jnp.exp(cs_pair[i]) for i in idx]
        for i in idx:
            y = y_diag[i] + y_off[i]
            if has_prev:
                y = y + yprev_ref[0, rows, lanes[i]] + dskip_ref[:, lanes[i]] * xs[i]
            y_ref[0, rows, lanes[i]] = y
        scale = [jnp.where(row_lo, dd_t[2 * p:2 * p + 1, :], dd_t[2 * p + 1:2 * p + 2, :]) for p in ps]
        upd = [_dot((xs[i].T * scale[i]).astype(BF16), bm) for i in idx]
        for i, p in enumerate(ps):
            end_col = jnp.where(row_lo, cs_t[2 * p:2 * p + 1, last:last + 1], cs_t[2 * p + 1:2 * p + 2, last:last + 1])
            s_ref[p] = s_pair[i] * jnp.exp(end_col) + upd[i]

    @pl.when(c == pl.num_programs(1) - 1)
    def _():
        sfin_ref[0] = s_ref[...]


def _ssd_prep_kernel(raw_ref, alog_ref, dtb_ref, cs_ref, cst_ref, dtt_ref, ddt_ref, *, nchunks):
    q = SSM_CHUNK
    ii = lax.broadcasted_iota(jnp.int32, (q, q), 0)
    jj = lax.broadcasted_iota(jnp.int32, (q, q), 1)
    tri = [[(jj <= ii).astype(BF16)], [(jj >= ii).astype(BF16)]]
    last = [q - 1, 0]
    chains = [(d, ck) for d in range(2) for ck in range(nchunks)]
    rows = [slice(ck * q, (ck + 1) * q) for _, ck in chains]
    dt = [_softplus(raw_ref[0, rows[i], :] + dtb_ref[d:d + 1, :]) for i, (d, _) in enumerate(chains)]
    a = [dt[i] * (-jnp.exp(alog_ref[d:d + 1, :])) for i, (d, _) in enumerate(chains)]
    cs = [_mm(tri[d], _pieces(a[i], 3)) for i, (d, _) in enumerate(chains)]
    dd = [dt[i] * jnp.exp(cs[i][last[d]:last[d] + 1, :] - cs[i]) for i, (d, _) in enumerate(chains)]
    for i, (d, ck) in enumerate(chains):
        cs_ref[d, 0, rows[i], :] = cs[i]
        cst_ref[d, 0, ck] = cs[i].T
        dtt_ref[d, 0, ck] = dt[i].T
        ddt_ref[d, 0, ck] = dd[i].T


def _ssd_prep(p_dt3, alog2, dtb2):
    b, l, _ = p_dt3.shape
    rows = min(l, 8 * SSM_CHUNK)
    nck = rows // SSM_CHUNK
    dt_blk = 0
    t_spec = pl.BlockSpec((2, 1, nck, SSM_CHUNK, LANES), lambda i, j: (0, i, j, 0, 0))
    t_shape = jax.ShapeDtypeStruct((2, b, l // SSM_CHUNK, SSM_CHUNK, LANES), F32)
    return pl.pallas_call(
        functools.partial(_ssd_prep_kernel, nchunks=nck),
        grid=(b, l // rows),
        in_specs=[pl.BlockSpec((1, rows, LANES), lambda i, j: (i, j, dt_blk)),
                  pl.BlockSpec((2, LANES), lambda i, j: (0, 0)),
                  pl.BlockSpec((2, LANES), lambda i, j: (0, 0))],
        out_specs=[pl.BlockSpec((2, 1, rows, LANES), lambda i, j: (0, i, j, 0)), t_spec, t_spec, t_spec],
        out_shape=[jax.ShapeDtypeStruct((2, b, l, LANES), F32), t_shape, t_shape, t_shape],
        compiler_params=_cparams(("arbitrary", "arbitrary")),
        name="ssd_prep",
    )(p_dt3, alog2, dtb2)


def _ssd_scan(xbc, prep, s0, yprev, dskip, rev):
    b, l, _ = xbc.shape
    n_sub = min(SSD_SUB, l // SSM_CHUNK)
    q = SSM_CHUNK * n_sub
    nc = l // q
    cidx = (lambda c: nc - 1 - c) if rev else (lambda c: c)
    d = 1 if rev else 0
    t_spec = pl.BlockSpec((1, 1, n_sub, SSM_CHUNK, LANES), lambda i, c: (d, i, cidx(c), 0, 0))
    in_specs = [pl.BlockSpec((1, q, SSM_D_INNER), lambda i, c: (i, cidx(c), 0)),
                pl.BlockSpec((1, q, 512), lambda i, c: (i, cidx(c), SSM_D_INNER // 512)),
                pl.BlockSpec((1, q, 512), lambda i, c: (i, cidx(c), SSM_D_INNER // 512 + 1)),
                pl.BlockSpec((1, 1, q, LANES), lambda i, c: (d, i, cidx(c), 0)),
                t_spec, t_spec, t_spec]
    args = [xbc, xbc, xbc, *prep]
    if s0 is not None:
        in_specs.append(pl.BlockSpec((1, SSM_PAIRS, LANES, SSM_STATE), lambda i, c: (i, 0, 0, 0)))
        args.append(s0)
    if yprev is not None:
        in_specs.append(pl.BlockSpec((1, q, SSM_D_INNER), lambda i, c: (i, cidx(c), 0)))
        in_specs.append(pl.BlockSpec((1, SSM_D_INNER), lambda i, c: (0, 0)))
        args += [yprev, dskip]
    kern = functools.partial(_ssd_kernel, rev=rev, has_init=s0 is not None, has_prev=yprev is not None, n_sub=n_sub)
    return pl.pallas_call(
        kern,
        grid=(b, nc),
        in_specs=in_specs,
        out_specs=[pl.BlockSpec((1, q, SSM_D_INNER), lambda i, c: (i, cidx(c), 0)),
                   pl.BlockSpec((1, SSM_PAIRS, LANES, SSM_STATE), lambda i, c: (i, 0, 0, 0))],
        out_shape=[jax.ShapeDtypeStruct((b, l, SSM_D_INNER), F32),
                   jax.ShapeDtypeStruct((b, SSM_PAIRS, LANES, SSM_STATE), F32)],
        scratch_shapes=[pltpu.VMEM((SSM_PAIRS, LANES, SSM_STATE), F32)],
        compiler_params=_cparams(("arbitrary", "arbitrary")),
        name="ssd_bwd" if rev else "ssd_fwd",
    )(*args)


CUM_PIECES = 2


def _pieces(x, n):
    out = []
    for i in range(n):
        h = x.astype(BF16)
        out.append(h)
        if i + 1 < n:
            x = x - h.astype(F32)
    return out


def _mm(a_pieces, b_pieces, nt=False):
    dot = _dot_nt if nt else _dot
    depth = max(len(a_pieces), len(b_pieces))
    acc = None
    for i in reversed(range(len(a_pieces))):
        for j in reversed(range(len(b_pieces))):
            if i + j < depth:
                t = dot(a_pieces[i], b_pieces[j])
                acc = t if acc is None else acc + t
    return acc


def _block_diag(y, m_lo, m_hi):
    return jnp.concatenate([y * m_lo, y * m_hi], axis=0)


def _wkv_kernel(*refs, rev, has_init, final, grid_rows):
    refs = list(refs)
    (rw_ref, w0_ref, w2_ref, a0_ref, a2_ref, kk_ref, ka_ref) = refs[:7]
    pos = 7
    s0_ref = None
    if has_init:
        s0_ref = refs[pos]
        pos += 1
    of_ref = rk_ref = lnw_ref = lnb_ref = None
    if final:
        of_ref, rk_ref, lnw_ref, lnb_ref = refs[pos:pos + 4]
        pos += 4
    o_ref, sfin_ref, s_ref = refs[pos], refs[pos + 1], refs[pos + 2]

    n = WKV_CHUNK
    c = pl.program_id(1)

    @pl.when(c == 0)
    def _():
        if has_init:
            s_ref[...] = s0_ref[0]
        else:
            s_ref[...] = jnp.zeros_like(s_ref)

    xl = rw_ref[0, :, 3 * WKV_WIDTH:3 * WKV_WIDTH + LANES]
    a_full = _sigmoid(a0_ref[...] + _dot(xl.astype(BF16), a2_ref[...]))
    xw = w0_ref[...] + _dot(jnp.tanh(xl).astype(BF16), w2_ref[...])
    lw_full = -math.exp(-0.5) * _sigmoid(xw)

    ti = lax.broadcasted_iota(jnp.int32, (n, LANES), 0)
    li = lax.broadcasted_iota(jnp.int32, (n, LANES), 1)
    si = li & (WKV_HEADSIZE - 1)
    strict = (si > ti) if rev else (si < ti)
    incl = (si >= ti) if rev else (si <= ti)
    eye = (si == ti).astype(BF16)
    m_lo = (li < WKV_HEADSIZE).astype(BF16)
    m_hi = (li >= WKV_HEADSIZE).astype(BF16)
    tq = lax.broadcasted_iota(jnp.int32, (n, n), 0)
    sq = lax.broadcasted_iota(jnp.int32, (n, n), 1)
    tri = jnp.concatenate([((sq >= tq) if rev else (sq <= tq)).astype(BF16)] * CUM_PIECES, axis=1)
    r2 = lax.broadcasted_iota(jnp.int32, (LANES, LANES), 0)
    c2 = lax.broadcasted_iota(jnp.int32, (LANES, LANES), 1)
    same_head = (r2 < WKV_HEADSIZE) == (c2 < WKV_HEADSIZE)
    ones_bd = same_head.astype(BF16)
    last = 0 if rev else n - 1

    def level_mask(m):
        same = (ti >> (m.bit_length())) == (si >> (m.bit_length()))
        t_hi = (ti & m) != 0
        s_hi = (si & m) != 0
        if rev:
            return same & jnp.logical_not(t_hi) & s_hi
        return same & t_hi & jnp.logical_not(s_hi)

    levels = []
    m = 2
    while m < n:
        levels.append(level_mask(m).astype(BF16))
        m *= 2
    level1 = level_mask(1).astype(BF16)

    def bd(y):
        return _block_diag(y.astype(BF16), m_lo, m_hi)

    def seg_sum(x):
        return _dot(x.astype(BF16), ones_bd)

    def lane_blk(p, base=0):
        return slice(base + p * LANES, base + (p + 1) * LANES)

    def rows(j):
        return slice(j * n, (j + 1) * n)

    subs = list(range(WKV_SUB))[::-1] if rev else list(range(WKV_SUB))
    chains = [(j, p) for j in subs for p in range(N_PAIRS)]
    idx = range(len(chains))
    r = [rw_ref[0, rows(j), lane_blk(p)] for j, p in chains]
    k = [rw_ref[0, rows(j), lane_blk(p, WKV_WIDTH)] for j, p in chains]
    v = [rw_ref[0, rows(j), lane_blk(p, 2 * WKV_WIDTH)] for j, p in chains]
    a = [a_full[rows(j), lane_blk(p)] for j, p in chains]
    lw = [lw_full[rows(j), lane_blk(p)] for j, p in chains]
    kkr = [k[i] * kk_ref[p] for i, (j, p) in enumerate(chains)]
    ss = [seg_sum(q * q) for q in kkr]
    kk = [kkr[i] / jnp.maximum(jnp.sqrt(ss[i]), 1e-12) for i in idx]
    kmod = [k[i] * (1.0 + (a[i] - 1.0) * ka_ref[p]) for i, (j, p) in enumerate(chains)]
    kka = [kk[i] * a[i] for i in idx]

    lg = [_dot(tri, jnp.concatenate(_pieces(q, CUM_PIECES), axis=0)) for q in lw]
    g_inv = [jnp.exp(-q) for q in lg]
    rh = [r[i] * jnp.exp(lg[i]) for i in idx]
    kh = [kmod[i] * g_inv[i] for i in idx]
    ah = [kka[i] * g_inv[i] for i in idx]
    bh = [-kk[i] * jnp.exp(lg[i] - lw[i]) for i in idx]
    g_end = [jnp.exp(q[last:last + 1, :]) for q in lg]

    lhs = [jnp.concatenate([bh[i], rh[i]], axis=0).astype(BF16) for i in idx]
    rhs = [jnp.concatenate([bd(ah[i]), bd(kh[i])], axis=0) for i in idx]
    aak = [_dot_nt(lhs[i], rhs[i]) for i in idx]
    a_ab = [jnp.where(strict, q[:n, :LANES], 0.0).astype(BF16) for q in aak]
    a_ra = [jnp.where(incl, q[n:, :LANES], 0.0).astype(BF16) for q in aak]
    a_bkrk = [jnp.concatenate([jnp.where(strict, q[:n, LANES:], 0.0), jnp.where(incl, q[n:, LANES:], 0.0)],
                              axis=0).astype(BF16) for q in aak]

    x = [eye + q * level1 for q in a_ab]
    for lm in levels:
        pm = [_dot(x[i], bd(a_ab[i] * lm)).astype(BF16) for i in idx]
        x = [x[i] + _dot(pm[i], bd(x[i])).astype(BF16) for i in idx]
    gv = [_dot(a_bkrk[i], bd(v[i])) for i in idx]
    ake = [(jnp.concatenate([ah[i], kh[i]], axis=0) * g_end[i]).astype(BF16) for i in idx]

    o = [None] * len(chains)
    for jj in range(WKV_SUB):
        ids = list(range(jj * N_PAIRS, (jj + 1) * N_PAIRS))
        s_bd = {i: s_ref[chains[i][1]] for i in ids}
        x0r = {i: _dot_nt(lhs[i], s_bd[i].astype(BF16)) for i in ids}
        u = {i: _dot(x[i], bd(x0r[i][:n] + gv[i][:n])) for i in ids}
        for i in ids:
            o[i] = x0r[i][n:] + gv[i][n:] + _dot(a_ra[i], bd(u[i]))
        uvt = {i: jnp.concatenate([u[i], v[i]], axis=0).T.astype(BF16) for i in ids}
        upd = {i: _dot(uvt[i], ake[i]) for i in ids}
        for i in ids:
            s_ref[chains[i][1]] = s_bd[i] * g_end[i] + jnp.where(same_head, upd[i], 0.0)

    if final:
        o = [o[i] + of_ref[0, p, rows(j), :] for i, (j, p) in enumerate(chains)]
        mu = [seg_sum(q) * (1.0 / WKV_HEADSIZE) for q in o]
        d = [o[i] - mu[i] for i in idx]
        var = [seg_sum(q * q) * (1.0 / WKV_HEADSIZE) for q in d]
        bonus = [seg_sum(r[i] * kmod[i] * rk_ref[p]) * v[i] for i, (j, p) in enumerate(chains)]
        o = [d[i] * lax.rsqrt(var[i] + WKV_GN_EPS) * lnw_ref[p] + lnb_ref[p] + bonus[i]
             for i, (j, p) in enumerate(chains)]
    if grid_rows is None:
        for i, (j, p) in enumerate(chains):
            o_ref[0, p, rows(j), :] = o[i]
    else:
        step = (pl.num_programs(1) - 1 - c) if rev else c
        cols_per_sub = n // grid_rows
        for i, (j, p) in enumerate(chains):
            for wl in range(cols_per_sub):
                w = (step * WKV_SUB + j) * cols_per_sub + wl
                o_ref[0, p, pl.ds(w, grid_rows, stride=GRID_W), :] = o[i][wl * grid_rows:(wl + 1) * grid_rows]

    @pl.when(c == pl.num_programs(1) - 1)
    def _():
        sfin_ref[0] = s_ref[...]


def _pairs(vec):
    return vec.reshape(N_PAIRS, 1, LANES)


def _wkv_scan(rw_s, w0, w2p, a0, a2p, k_k, k_a, s0, o_f, r_k, ln_w, ln_b, rev, to_row_major=False):
    b, l, w = rw_s.shape
    n = WKV_CHUNK * WKV_SUB
    nc = l // n
    cidx = (lambda c: nc - 1 - c) if rev else (lambda c: c)
    final = o_f is not None
    vec_spec = pl.BlockSpec((N_PAIRS, 1, LANES), lambda i, c: (0, 0, 0))
    st_spec = pl.BlockSpec((1, N_PAIRS, LANES, LANES), lambda i, c: (i, 0, 0, 0))
    in_specs = [pl.BlockSpec((1, n, w), lambda i, c: (i, cidx(c), 0)),
                pl.BlockSpec((1, WKV_WIDTH), lambda i, c: (0, 0)),
                pl.BlockSpec((LANES, WKV_WIDTH), lambda i, c: (0, 0)),
                pl.BlockSpec((1, WKV_WIDTH), lambda i, c: (0, 0)),
                pl.BlockSpec((LANES, WKV_WIDTH), lambda i, c: (0, 0)),
                vec_spec, vec_spec]
    args = [rw_s, w0.reshape(1, WKV_WIDTH), w2p, a0.reshape(1, WKV_WIDTH), a2p, _pairs(k_k), _pairs(k_a)]
    if s0 is not None:
        in_specs.append(st_spec)
        args.append(s0)
    if final:
        in_specs += [pl.BlockSpec((1, N_PAIRS, n, LANES), lambda i, c: (i, 0, cidx(c), 0)),
                     vec_spec, vec_spec, vec_spec]
        args += [o_f, _pairs(r_k), _pairs(ln_w), _pairs(ln_b)]
    o_spec = pl.BlockSpec((1, N_PAIRS, n, LANES), lambda i, c: (i, 0, cidx(c), 0))
    grid_rows = None
    if to_row_major:
        o_spec = pl.BlockSpec((1, N_PAIRS, l, LANES), lambda i, c: (i, 0, 0, 0))
        grid_rows = l // GRID_W
        assert WKV_CHUNK % grid_rows == 0
    kern = functools.partial(_wkv_kernel, rev=rev, has_init=s0 is not None, final=final, grid_rows=grid_rows)
    return pl.pallas_call(
        kern,
        grid=(b, nc),
        in_specs=in_specs,
        out_specs=[o_spec, st_spec],
        out_shape=[jax.ShapeDtypeStruct((b, N_PAIRS, l, LANES), F32),
                   jax.ShapeDtypeStruct((b, N_PAIRS, LANES, LANES), F32)],
        scratch_shapes=[pltpu.VMEM((N_PAIRS, LANES, LANES), F32)],
        compiler_params=_cparams(("arbitrary", "arbitrary")),
        name="wkv_bwd" if rev else "wkv_fwd",
    )(*args)


def _final_kernel(y_ref, za_ref, o_ref, zb_ref, ga_ref, gb_ref, x_ref, mod_ref,
                  ng_ref, pa_ref, pb_ref, wo_ref, fg_ref, out_ref):
    y = y_ref[...] * za_ref[...].astype(F32)
    y = y * lax.rsqrt(jnp.mean(y * y, axis=-1, keepdims=True) + NORM_EPS) * ng_ref[...]
    u_a = _dot(y.astype(BF16), pa_ref[...])
    o = jnp.concatenate([jnp.concatenate([o_ref[s, p] for p in range(N_PAIRS)], axis=1)
                         for s in range(o_ref.shape[0])], axis=0)
    u_b = _dot((o * zb_ref[...].astype(F32)).astype(BF16), pb_ref[...])
    m = ga_ref[...].astype(F32) * u_a + gb_ref[...].astype(F32) * u_b
    out = _dot(m.astype(BF16), wo_ref[...])
    gate = mod_ref[0][:, 2 * D_MODEL:]
    xo = x_ref[...] + gate * out
    out_ref[...] = xo * lax.rsqrt(jnp.mean(xo * xo, axis=-1, keepdims=True) + NORM_EPS) * fg_ref[...]


def _final(y2, p_plain, o4, x2, mod3, ssm_norm_g, p_a, p_b, w_out, final_g, row0, tiles_per_row, tm):
    t = x2.shape[0]
    w1 = D_MODEL
    l = o4.shape[2]
    o_rows = min(tm, l)
    tiles_per_seq = l // o_rows

    def resident(shape):
        return pl.BlockSpec(shape, lambda i: (0, 0), pipeline_mode=pl.Buffered(1))

    return pl.pallas_call(
        _final_kernel,
        grid=(t // tm,),
        in_specs=[pl.BlockSpec((tm, SSM_D_INNER), lambda i: (i, 0)),
                  pl.BlockSpec((tm, SSM_D_INNER), lambda i: (i, 0)),
                  pl.BlockSpec((tm // o_rows, N_PAIRS, o_rows, LANES),
                               lambda i: (i // tiles_per_seq, 0, i % tiles_per_seq, 0)),
                  pl.BlockSpec((tm, w1), lambda i: (i, ZB_OFF // w1)),
                  pl.BlockSpec((tm, w1), lambda i: (i, GL_OFF // w1)),
                  pl.BlockSpec((tm, w1), lambda i: (i, GL_OFF // w1 + 1)),
                  pl.BlockSpec((tm, w1), lambda i: (i, 0)),
                  pl.BlockSpec((1, 1, 3 * D_MODEL), lambda i: (row0 + i // tiles_per_row, 0, 0)),
                  resident((1, SSM_D_INNER)),
                  resident((SSM_D_INNER, D_MODEL)),
                  resident((WKV_WIDTH, D_MODEL)),
                  resident((D_MODEL, D_MODEL)),
                  resident((1, D_MODEL))],
        out_specs=pl.BlockSpec((tm, D_MODEL), lambda i: (i, 0)),
        out_shape=jax.ShapeDtypeStruct((t, D_MODEL), F32),
        compiler_params=_cparams(("arbitrary",), MERGE_VMEM_LIMIT),
        name="merge_out",
    )(y2, p_plain, o4, p_plain, p_plain, p_plain, x2, mod3,
      ssm_norm_g.reshape(1, SSM_D_INNER), p_a, p_b, w_out, final_g.reshape(1, D_MODEL))


def _wkv_state_to_pairs(s):
    b = s.shape[0]
    s = s.reshape(b, N_PAIRS, 2, WKV_HEADSIZE, WKV_HEADSIZE)
    z = jnp.zeros_like(s[:, :, 0])
    top = jnp.concatenate([s[:, :, 0], z], axis=-1)
    bot = jnp.concatenate([z, s[:, :, 1]], axis=-1)
    return jnp.concatenate([top, bot], axis=-2)


def _wkv_state_from_pairs(s):
    b = s.shape[0]
    h = WKV_HEADSIZE
    return jnp.stack([s[:, :, :h, :h], s[:, :, h:, h:]], axis=2).reshape(b, WKV_HEADS, h, h)


def _group(x, mod3, row0, grid, states, wts):
    b, l, _ = x.shape
    t = b * l
    x2 = x.reshape(t, D_MODEL)
    if grid:
        h, h_rw = _adaln_grid(x, mod3, wts["norm_g"], row0)
    else:
        h = h_rw = _adaln(x2, mod3, wts["norm_g"], row0, t // ADALN_TM, ADALN_TM)
    p_plain, p_dt = _project_gates(h, wts["w_plain"])
    p_dt3 = p_dt.reshape(b, l, PLAIN_W - DT_OFF)

    xbc = _project(h, wts["w_xbc"], "conv", l,
                   (wts["conv_w"], wts["conv_b"].reshape(1, XBC_W))).reshape(b, l, XBC_W)
    s_f = s_b = None
    if states is not None:
        s_f = states[0].reshape(b, SSM_PAIRS, LANES, SSM_STATE)
        s_b = states[1].reshape(b, SSM_PAIRS, LANES, SSM_STATE)
    prep = _ssd_prep(p_dt3, wts["alog"], wts["dtb"])
    y_f, fs_f = _ssd_scan(xbc, prep, s_f, None, None, rev=False)
    y, fs_b = _ssd_scan(xbc, prep, s_b, y_f, wts["dskip"], rev=True)

    rw_s = _project(h_rw, wts["w_rw"], "shift", l,
                    (wts["shift_mu"].reshape(1, RWKV_SCAN_W),)).reshape(b, l, RWKV_SCAN_W)
    w_f = w_b = None
    if states is not None:
        w_f = _wkv_state_to_pairs(states[2])
        w_b = _wkv_state_to_pairs(states[3])
    o_f, fw_f = _wkv_scan(rw_s, wts["w0"][0], wts["w2p"][0], wts["a0"], wts["a2p"], wts["k_k"], wts["k_a"],
                          w_f, None, None, None, None, rev=False)
    o, fw_b = _wkv_scan(rw_s, wts["w0"][1], wts["w2p"][1], wts["a0"], wts["a2p"], wts["k_k"], wts["k_a"],
                        w_b, o_f, wts["r_k"], wts["ln_w"], wts["ln_b"], rev=True)
    if grid:
        rows = l // GRID_W
        o = o.reshape(b, N_PAIRS, GRID_W, rows, LANES).transpose(0, 1, 3, 2, 4).reshape(b, N_PAIRS, l, LANES)

    out = _final(y.reshape(t, SSM_D_INNER), p_plain, o, x2, mod3, wts["ssm_norm_g"],
                 wts["p_a"], wts["p_b"], wts["w_out"], wts["final_g"], row0,
                 (l if grid else t) // FINAL_TM, FINAL_TM)
    finals = (fs_f.reshape(b, SSM_HEADS, SSM_HEADDIM, SSM_STATE), fs_b.reshape(b, SSM_HEADS, SSM_HEADDIM, SSM_STATE),
              _wkv_state_from_pairs(fw_f), _wkv_state_from_pairs(fw_b))
    return out.reshape(b, l, D_MODEL), finals


def kernel(x_prompt, x_sample, state_ssm_fwd, state_ssm_bwd, state_wkv_fwd, state_wkv_bwd, c, c_ctx, w_mod, b_mod, norm_g, w_in, conv_w, conv_b, a_log, dt_bias, d_skip, ssm_norm_g, p_a, shift_mu, w0, w2, a0, a2, k_k, k_a, r_k, ln_w, ln_b, p_b, w_out, final_g):
    depth = w_mod.shape[0]
    assert depth == 1, "single-layer stack only"
    l0 = 0
    w_in0 = w_in[l0].astype(BF16)
    zpad = jnp.zeros((WKV_RANK, WKV_WIDTH), F32)
    w_plain = jnp.concatenate([w_in0[:, :ZA_END], w_in0[:, ZB_END:], w_in0[:, RW_END:ZB_END], w_in0[:, XBC_END:DT_END],
                               jnp.zeros((D_MODEL, PLAIN_W - DT_OFF - SSM_HEADS), BF16)], axis=1)
    wts = {
        "norm_g": norm_g[l0],
        "w_plain": w_plain,
        "w_xbc": w_in0[:, ZA_END:XBC_END],
        "w_rw": w_in0[:, DT_END:RW_END],
        "conv_w": conv_w[l0], "conv_b": conv_b[l0],
        "alog": jnp.pad(a_log[l0], ((0, 0), (0, LANES - SSM_HEADS))),
        "dtb": jnp.pad(dt_bias[l0], ((0, 0), (0, LANES - SSM_HEADS))),
        "dskip": jnp.repeat(d_skip[l0], SSM_HEADDIM).reshape(1, SSM_D_INNER),
        "ssm_norm_g": ssm_norm_g[l0],
        "p_a": p_a[l0].astype(BF16), "p_b": p_b[l0].astype(BF16), "w_out": w_out[l0].astype(BF16),
        "shift_mu": shift_mu[l0],
        "w0": w0[l0],
        "w2p": [jnp.concatenate([w2[l0, d], zpad], axis=0).astype(BF16) for d in range(2)],
        "a0": a0[l0],
        "a2p": jnp.concatenate([zpad, a2[l0]], axis=0).astype(BF16),
        "k_k": k_k[l0], "k_a": k_a[l0], "r_k": r_k[l0], "ln_w": ln_w[l0], "ln_b": ln_b[l0],
        "final_g": final_g,
    }
    nb = c.shape[0]
    cond8 = jnp.concatenate([c_ctx[None, :], c, jnp.zeros((8 - 1 - nb, D_MODEL), F32)], axis=0)
    mod3 = _modulation(cond8, w_mod[l0], b_mod[l0]).reshape(8, 1, 3 * D_MODEL)

    y_prompt, (sf, sb, wf, wb) = _group(x_prompt, mod3, 0, False, None, wts)
    lat_states = (state_ssm_fwd[:, l0], state_ssm_bwd[:, l0], state_wkv_fwd[:, l0], state_wkv_bwd[:, l0])
    y_sample, _ = _group(x_sample, mod3, 1, True, lat_states, wts)
    return (y_prompt, y_sample, sf[:, None], sb[:, None], wf[:, None], wb[:, None])
```
